```python
import math
import jax, jax.numpy as jnp
from jax import lax
import numpy as np

D_MODEL = 1024
BATCH = 8
SEQ = 2048
DEPTH = 2
DEC_BATCH = 32
DEC_SEQ = 1
PAST_LEN = 8192
PAGE_SIZE = 128

N_META = 16
D_MIX = D_MODEL
D_REC = D_MIX // 2
D_ATT = D_MIX - D_REC
REC_HEAD_DIM = 128
N_REC_HEADS = D_REC // REC_HEAD_DIM
ATT_HEAD_DIM = 64
N_ATT_HEADS = D_ATT // (2 * ATT_HEAD_DIM)
D_IN = 4 * D_REC + 4 * D_ATT
CHUNK = 64
Q_BLOCK = 128
EPS = 1e-6

kernel_name = "hymba_hgrn2_diffattn_step"

F32 = jnp.float32


def rms_norm(x, w):
    xf = x.astype(F32)
    y = xf * lax.rsqrt(jnp.mean(xf * xf, axis=-1, keepdims=True) + EPS)
    return (y * w.astype(F32)).astype(x.dtype)


def lower_bounds(lb_logits):
    p = jax.nn.softmax(lb_logits.astype(F32), axis=0)
    c = jnp.cumsum(p, axis=0)
    return c - c[0:1]


def hgrn2_chunk(S, inp):
    q, k, v, g = inp
    C = q.shape[2]
    b = jnp.cumsum(g, axis=2)
    causal = jnp.tril(jnp.ones((C, C), dtype=bool))
    diff = b[:, :, :, None, :] - b[:, :, None, :, :]
    decay = jnp.exp(jnp.where(causal[None, None, :, :, None], diff, -jnp.inf))
    a = jnp.einsum('bhtc,bhsc,bhtsc->bhts', q, k, decay)
    o = (jnp.einsum('bhts,bhsv->bhtv', a, v)
         + jnp.einsum('bhtc,bhcv->bhtv', q * jnp.exp(b), S))
    b_last = b[:, :, -1:, :]
    S_new = (jnp.exp(b_last[:, :, 0, :])[..., None] * S
             + jnp.einsum('bhsc,bhsv->bhcv', k * jnp.exp(b_last - b), v))
    return S_new, o


def hgrn2_scan(q, k, v, g, S0):
    B, H, T, _ = q.shape
    c = min(CHUNK, T)
    n = -(-T // c)
    pad = n * c - T

    def prep(a):
        a = jnp.pad(a, ((0, 0), (0, 0), (0, pad), (0, 0)))
        return jnp.moveaxis(a.reshape(B, H, n, c, a.shape[-1]), 2, 0)

    S, o = lax.scan(hgrn2_chunk, S0.astype(F32), (prep(q), prep(k), prep(v), prep(g)))
    o = jnp.moveaxis(o, 0, 2).reshape(B, H, n * c, -1)[:, :, :T]
    return o, S.astype(S0.dtype)


def diff_attention(q, k, v, q_pos, k_pos, lam):
    B, Tq, H, _, Dh = q.shape
    bq = min(Q_BLOCK, Tq)
    n = -(-Tq // bq)
    pad = n * bq - Tq
    qp = jnp.pad(q, ((0, 0), (0, pad), (0, 0), (0, 0), (0, 0)))
    posp = jnp.pad(q_pos, (0, pad), mode='edge')
    qb = jnp.moveaxis(qp.reshape(B, n, bq, H, 2, Dh), 1, 0)
    pb = posp.reshape(n, bq)
    kf = k.astype(F32)
    vf = v.astype(F32)
    scale = Dh ** -0.5

    def block(args):
        qi, pi = args
        s = jnp.einsum('bqhcd,bkhcd->bhcqk', qi.astype(F32), kf) * scale
        mask = k_pos[None, :] <= pi[:, None]
        p = jax.nn.softmax(jnp.where(mask, s, -jnp.inf), axis=-1)
        a = p[:, :, 0] - lam * p[:, :, 1]
        return jnp.einsum('bhqk,bkhv->bqhv', a, vf)

    o = lax.map(block, (qb, pb))
    return jnp.moveaxis(o, 0, 1).reshape(B, n * bq, H, -1)[:, :Tq]


def mixer_layer(h, l, S0, k_past, v_past, front_pad, norm_w, w_in, w_out, lb,
                hgrn_norm_w, subln_w, lam_q1, lam_k1, lam_q2, lam_k2):
    B, T, _ = h.shape
    xn = rms_norm(h, norm_w[l])
    proj = jnp.einsum('btd,de->bte', xn, w_in[l])
    rq, rf, ri, rg, aq, ak, av, ag = jnp.split(
        proj, [D_REC, 2 * D_REC, 3 * D_REC, 4 * D_REC, 4 * D_REC + D_ATT,
               4 * D_REC + 2 * D_ATT, 4 * D_REC + 3 * D_ATT], axis=-1)

    lbl = lb[l]
    zf = rf.astype(F32)
    log_f = jnp.logaddexp(jnp.log(lbl), jnp.log1p(-lbl) + jax.nn.log_sigmoid(zf))
    k_in = (1.0 - lbl) * jax.nn.sigmoid(-zf)

    def rec_heads(a):
        a = a.astype(F32).reshape(B, T, N_REC_HEADS, REC_HEAD_DIM).transpose(0, 2, 1, 3)
        return jnp.pad(a, ((0, 0), (0, 0), (front_pad, 0), (0, 0)))

    o_rec, S_new = hgrn2_scan(rec_heads(rq), rec_heads(k_in), rec_heads(ri), rec_heads(log_f), S0)
    o_rec = o_rec[:, :, front_pad:].transpose(0, 2, 1, 3)
    o_rec = rms_norm(o_rec, hgrn_norm_w[l]).reshape(B, T, D_REC)
    y_rec = (o_rec * jax.nn.silu(rg.astype(F32))).astype(h.dtype)

    q = aq.reshape(B, T, N_ATT_HEADS, 2, ATT_HEAD_DIM)
    k_new = ak.reshape(B, T, N_ATT_HEADS, 2 * ATT_HEAD_DIM)
    v_new = av.reshape(B, T, N_ATT_HEADS, 2 * ATT_HEAD_DIM)
    if k_past is None:
        k_all, v_all, P = k_new, v_new, 0
    else:
        P = k_past.shape[1]
        k_all = jnp.concatenate([k_past.astype(k_new.dtype), k_new], axis=1)
        v_all = jnp.concatenate([v_past.astype(v_new.dtype), v_new], axis=1)
    q_pos = P + jnp.arange(T, dtype=jnp.int32)
    k_pos = jnp.arange(P + T, dtype=jnp.int32)
    lam_init = 0.8 - 0.6 * math.exp(-0.3 * l)
    lam = (jnp.exp(jnp.sum(lam_q1[l].astype(F32) * lam_k1[l].astype(F32)))
           - jnp.exp(jnp.sum(lam_q2[l].astype(F32) * lam_k2[l].astype(F32))) + lam_init)
    o_att = diff_attention(q, k_all.reshape(B, P + T, N_ATT_HEADS, 2, ATT_HEAD_DIM),
                           v_all, q_pos, k_pos, lam)
    o_att = rms_norm(o_att, subln_w[l]) * (1.0 - lam_init)
    y_att = (o_att.reshape(B, T, D_ATT) * jax.nn.silu(ag.astype(F32))).astype(h.dtype)

    out = jnp.einsum('bte,ed->btd', jnp.concatenate([y_rec, y_att], axis=-1), w_out[l])
    return h + out, S_new, k_new, v_new


def setup_inputs(seed: int = 0) -> dict:
    key = jax.random.key(seed)
    ks = jax.random.split(key, 20)
    n_pages = PAST_LEN // PAGE_SIZE
    n_used = DEC_BATCH * n_pages
    n_phys = n_used + max(1, n_used // 4)
    kv_shape = (DEPTH, n_phys, PAGE_SIZE, N_ATT_HEADS, 2 * ATT_HEAD_DIM)
    nrm = lambda k, s: jax.random.normal(k, s, F32)
    return {
        "x_prompt": nrm(ks[0], (BATCH, SEQ, D_MODEL)),
        "x_sample": nrm(ks[1], (DEC_BATCH, DEC_SEQ, D_MODEL)),
        "cache_k": nrm(ks[2], kv_shape),
        "cache_v": nrm(ks[3], kv_shape),
        "state_hgrn": 0.3 * nrm(ks[4], (DEPTH, DEC_BATCH, N_REC_HEADS, REC_HEAD_DIM, REC_HEAD_DIM)),
        "page_table": jax.random.permutation(ks[5], n_phys)[:n_used].reshape(DEC_BATCH, n_pages).astype(jnp.int32),
        "meta_tokens": nrm(ks[6], (N_META, D_MODEL)),
        "norm_w": 1.0 + 0.02 * nrm(ks[7], (DEPTH, D_MODEL)),
        "w_in": nrm(ks[8], (DEPTH, D_MODEL, D_IN)) * D_MODEL ** -0.5,
        "w_out": nrm(ks[9], (DEPTH, D_MIX, D_MODEL)) * D_MIX ** -0.5,
        "lb_logits": nrm(ks[10], (DEPTH, D_REC)),
        "hgrn_norm_w": 1.0 + 0.02 * nrm(ks[11], (DEPTH, REC_HEAD_DIM)),
        "subln_w": 1.0 + 0.02 * nrm(ks[12], (DEPTH, 2 * ATT_HEAD_DIM)),
        "lam_q1": 0.1 * nrm(ks[13], (DEPTH, ATT_HEAD_DIM)),
        "lam_k1": 0.1 * nrm(ks[14], (DEPTH, ATT_HEAD_DIM)),
        "lam_q2": 0.1 * nrm(ks[15], (DEPTH, ATT_HEAD_DIM)),
        "lam_k2": 0.1 * nrm(ks[16], (DEPTH, ATT_HEAD_DIM)),
        "final_norm_w": 1.0 + 0.02 * nrm(ks[17], (D_MODEL,)),
    }


def reference(x_prompt, x_sample, cache_k, cache_v, state_hgrn, page_table, meta_tokens,
              norm_w, w_in, w_out, lb_logits, hgrn_norm_w, subln_w,
              lam_q1, lam_k1, lam_q2, lam_k2, final_norm_w):
    lb = lower_bounds(lb_logits)
    n_seq, n_pages = page_table.shape
    front_pad = (-N_META) % CHUNK
    bp = x_prompt.shape[0]
    meta = jnp.broadcast_to(meta_tokens[None].astype(x_prompt.dtype), (bp, N_META, D_MODEL))
    hp = jnp.concatenate([meta, x_prompt], axis=1)
    hs = x_sample
    kp, vp, sp, ksm, vsm, ssm = [], [], [], [], [], []
    for l in range(DEPTH):
        S0p = jnp.zeros((bp, N_REC_HEADS, REC_HEAD_DIM, REC_HEAD_DIM), state_hgrn.dtype)
        hp, S_p, k_p, v_p = mixer_layer(hp, l, S0p, None, None, front_pad, norm_w, w_in, w_out, lb,
                                        hgrn_norm_w, subln_w, lam_q1, lam_k1, lam_q2, lam_k2)
        k_past = cache_k[l][page_table].reshape(n_seq, n_pages * PAGE_SIZE, N_ATT_HEADS, 2 * ATT_HEAD_DIM)
        v_past = cache_v[l][page_table].reshape(n_seq, n_pages * PAGE_SIZE, N_ATT_HEADS, 2 * ATT_HEAD_DIM)
        hs, S_s, k_s, v_s = mixer_layer(hs, l, state_hgrn[l], k_past, v_past, 0, norm_w, w_in, w_out, lb,
                                        hgrn_norm_w, subln_w, lam_q1, lam_k1, lam_q2, lam_k2)
        kp.append(k_p); vp.append(v_p); sp.append(S_p)
        ksm.append(k_s); vsm.append(v_s); ssm.append(S_s)
    y_prompt = rms_norm(hp, final_norm_w)[:, N_META:]
    y_sample = rms_norm(hs, final_norm_w)
    new_k_prompt = jnp.stack(kp)
    new_v_prompt = jnp.stack(vp)
    new_state_prompt = jnp.stack(sp)
    new_k_sample = jnp.stack(ksm)
    new_v_sample = jnp.stack(vsm)
    new_state_sample = jnp.stack(ssm)
    return (y_prompt, y_sample, new_k_prompt, new_v_prompt, new_state_prompt,
            new_k_sample, new_v_sample, new_state_sample)
```

```python
import functools
import math

import jax
import jax.numpy as jnp
from jax import lax
from jax.experimental import pallas as pl
from jax.experimental.pallas import tpu as pltpu

F32 = jnp.float32
BF16 = jnp.bfloat16

D_MODEL = 1024
N_META = 16
GROUP = 512
HEAD = 128
N_HEADS = 4
ATT_DH = 64
PAGE = 128
CHUNK = 64
SUB = 16
EPS = 1e-6
VMEM_LIMIT = 52 * 1024 * 1024

NT_DIMS = (((1,), (1,)), ((), ()))
TN_DIMS = (((0,), (0,)), ((), ()))


def _lam_init(layer):
    return 0.8 - 0.6 * math.exp(-0.3 * layer)


def _sigmoid(x):
    return 1.0 / (1.0 + jnp.exp(-x))


def _inproj_kernel(*refs, aliased):
    if aliased:
        refs = refs[:6] + refs[8:]
    (x_ref, nw_ref, w_ref, loglb_ref, log1mlb_ref, omlb_ref,
     rq_ref, logf_ref, kin_ref, ri_ref, grec_ref, aq_ref, k_ref, v_ref, kbf_ref, vbf_ref,
     gatt_ref) = refs
    x = x_ref[...]
    ms = jnp.mean(x * x, axis=-1, keepdims=True)
    xn = (x * lax.rsqrt(ms + EPS) * nw_ref[...]).astype(BF16)

    def proj(g):
        return jnp.dot(xn, w_ref[:, g * GROUP:(g + 1) * GROUP], preferred_element_type=F32)

    rq_ref[...] = proj(0).astype(BF16)
    z = proj(1)
    log_sig = jnp.minimum(z, 0.0) - jnp.log1p(jnp.exp(-jnp.abs(z)))
    a = loglb_ref[...]
    b = log1mlb_ref[...] + log_sig
    logf_ref[...] = jnp.maximum(a, b) + jnp.log1p(jnp.exp(-jnp.abs(a - b)))
    kin_ref[...] = omlb_ref[...] * _sigmoid(-z)
    ri_ref[...] = proj(2).astype(BF16)
    g = proj(3)
    grec_ref[...] = g * _sigmoid(g)
    aq_ref[...] = (proj(4) * (ATT_DH ** -0.5)).astype(BF16)
    k = proj(5)
    k_ref[...] = k
    kbf_ref[...] = k.astype(BF16)
    v = proj(6)
    v_ref[...] = v
    vbf_ref[...] = v.astype(BF16)
    g = proj(7)
    gatt_ref[...] = g * _sigmoid(g)


def _inproj(h, nw, w, loglb, log1mlb, omlb, layer, depth, kv_bufs, tm):
    m = h.shape[0]
    grid = (m // tm,)
    row = lambda i: (i, 0)
    const = lambda i: (0, 0)
    in_specs = [
        pl.BlockSpec((tm, D_MODEL), row),
        pl.BlockSpec((1, D_MODEL), const),
        pl.BlockSpec((D_MODEL, 8 * GROUP), const),
        pl.BlockSpec((1, GROUP), const),
        pl.BlockSpec((1, GROUP), const),
        pl.BlockSpec((1, GROUP), const),
    ]
    args = [h, nw, w, loglb, log1mlb, omlb]
    aliases = {}
    if kv_bufs is not None:
        in_specs += [pl.BlockSpec(memory_space=pl.ANY)] * 2
        args += list(kv_bufs)
        aliases = {6: 6, 7: 7}
    tile = lambda dt: jax.ShapeDtypeStruct((m, GROUP), dt)
    stacked = jax.ShapeDtypeStruct((depth, m, GROUP), F32)
    out_shape = [tile(BF16), tile(F32), tile(F32), tile(BF16), tile(F32), tile(BF16),
                 stacked, stacked, tile(BF16), tile(BF16), tile(F32)]
    tile_spec = pl.BlockSpec((tm, GROUP), row)
    stacked_spec = pl.BlockSpec((None, tm, GROUP), lambda i: (layer, i, 0))
    out_specs = [tile_spec] * 6 + [stacked_spec] * 2 + [tile_spec] * 3
    return pl.pallas_call(
        functools.partial(_inproj_kernel, aliased=kv_bufs is not None),
        grid=grid, in_specs=in_specs, out_specs=out_specs, out_shape=out_shape,
        input_output_aliases=aliases,
        compiler_params=pltpu.CompilerParams(
            dimension_semantics=("parallel",), vmem_limit_bytes=VMEM_LIMIT),
        name=f"inproj_m{m}",
    )(*args)


def _hgrn_kernel(q_ref, g_ref, k_ref, v_ref, gate_ref, s0_ref, nw_ref, y_ref, sout_ref, *, seq):
    first = seq % CHUNK
    n_full = seq // CHUNK
    nw = nw_ref[...]

    def chunk(start, c, st):
        sl = pl.ds(start, c)
        g = g_ref[sl, :]
        q = q_ref[sl, :].astype(F32)
        k = k_ref[sl, :]
        v = v_ref[sl, :]
        vf = v.astype(F32)
        r_i = lax.broadcasted_iota(jnp.int32, (c, c), 0)
        c_i = lax.broadcasted_iota(jnp.int32, (c, c), 1)
        tril = (r_i >= c_i).astype(F32)
        b = jnp.dot(tril, g, precision=lax.Precision.HIGHEST, preferred_element_type=F32)
        o = lax.dot_general((q * jnp.exp(b)).astype(BF16), st.astype(BF16), NT_DIMS,
                            preferred_element_type=F32)
        t_i = lax.broadcasted_iota(jnp.int32, (SUB, 1), 0)
        parts = []
        for s in range(c // SUB):
            lo = s * SUB
            b_s = b[lo:lo + SUB]
            q_s = q[lo:lo + SUB]
            k_s = k[lo:lo + SUB]
            v_s = vf[lo:lo + SUB]
            o_s = o[lo:lo + SUB]
            if s > 0:
                edge = b[lo - 1:lo]
                qh = (q_s * jnp.exp(b_s - edge)).astype(BF16)
                kh = (k[:lo] * jnp.exp(edge - b[:lo])).astype(BF16)
                a = lax.dot_general(qh, kh, NT_DIMS, preferred_element_type=F32)
                o_s = o_s + jnp.dot(a.astype(BF16), v[:lo], preferred_element_type=F32)
            for j in range(SUB):
                e = jnp.exp(jnp.minimum(b_s - b_s[j:j + 1], 0.0))
                col = jnp.sum(q_s * e * k_s[j:j + 1], axis=-1, keepdims=True)
                col = jnp.where(t_i >= j, col, 0.0)
                o_s = o_s + col * v_s[j:j + 1]
            parts.append(o_s)
        o = parts[0] if len(parts) == 1 else jnp.concatenate(parts, axis=0)
        ms = jnp.mean(o * o, axis=-1, keepdims=True)
        y = o * lax.rsqrt(ms + EPS) * nw * gate_ref[sl, :]
        y_ref[sl, :] = y.astype(y_ref.dtype)
        b_last = b[c - 1:c]
        kd = (k * jnp.exp(b_last - b)).astype(BF16)
        upd = lax.dot_general(v, kd, TN_DIMS, preferred_element_type=F32)
        return st * jnp.exp(b_last) + upd

    st = s0_ref[...].T
    if first:
        st = chunk(0, first, st)
    if n_full:
        st = lax.fori_loop(
            0, n_full,
            lambda j, s: chunk(pl.multiple_of(first + j * CHUNK, SUB), CHUNK, s), st)
    sout_ref[...] = st.T


def _hgrn(rq, logf, kin, ri, grec, s0, nw, name):
    bsz, seq, _ = rq.shape
    assert seq % SUB == 0
    tok = pl.BlockSpec((None, seq, HEAD), lambda b, h: (b, 0, h))
    state = pl.BlockSpec((None, None, HEAD, HEAD), lambda b, h: (b, h, 0, 0))
    return pl.pallas_call(
        functools.partial(_hgrn_kernel, seq=seq),
        grid=(bsz, N_HEADS),
        in_specs=[tok, tok, tok, tok, tok, state, pl.BlockSpec((1, HEAD), lambda b, h: (0, 0))],
        out_specs=[tok, state],
        out_shape=[jax.ShapeDtypeStruct((bsz, seq, GROUP), BF16),
                   jax.ShapeDtypeStruct(s0.shape, F32)],
        compiler_params=pltpu.CompilerParams(
            dimension_semantics=("parallel", "parallel"), vmem_limit_bytes=VMEM_LIMIT),
        name=name,
    )(rq, logf, kin, ri, grec, s0, nw)


def _lam(lamv, lam_init):
    return (jnp.exp(jnp.sum(lamv[0:1] * lamv[1:2], axis=-1, keepdims=True))
            - jnp.exp(jnp.sum(lamv[2:3] * lamv[3:4], axis=-1, keepdims=True)) + lam_init)


def _split_components(q):
    lane = lax.broadcasted_iota(jnp.int32, q.shape, 1)
    qf = q.astype(F32)
    return jnp.concatenate([jnp.where(lane < ATT_DH, qf, 0.0), jnp.where(lane >= ATT_DH, qf, 0.0)],
                           axis=0).astype(BF16)


def _attend(qp, kb, vb, carry, mask=None):
    m, l, acc = carry
    s = lax.dot_general(qp, kb, NT_DIMS, preferred_element_type=F32)
    if mask is not None:
        s = jnp.where(mask, s, -jnp.inf)
    m_new = jnp.maximum(m, jnp.max(s, axis=-1, keepdims=True))
    alpha = jnp.exp(m - m_new)
    p = jnp.exp(s - m_new)
    l = alpha * l + jnp.sum(p, axis=-1, keepdims=True)
    acc = alpha * acc + jnp.dot(p.astype(BF16), vb, preferred_element_type=F32)
    return m_new, l, acc


def _attn_prompt_kernel(q_ref, k_ref, v_ref, gate_ref, sw_ref, lamv_ref, y_ref, *, tq, lam_init):
    i = pl.program_id(2)
    lam = _lam(lamv_ref[...], lam_init)
    sw = sw_ref[...]

    def init(rows):
        return (jnp.full((rows, 1), -jnp.inf, F32), jnp.zeros((rows, 1), F32),
                jnp.zeros((rows, HEAD), F32))

    def causal(t):
        r_i = lax.broadcasted_iota(jnp.int32, (2 * t, t), 0)
        c_i = lax.broadcasted_iota(jnp.int32, (2 * t, t), 1)
        return c_i <= jnp.where(r_i >= t, r_i - t, r_i)

    def finish(carry, t, rows):
        _, l, acc = carry
        o = acc[:t] / l[:t] - lam * (acc[t:] / l[t:])
        ms = jnp.mean(o * o, axis=-1, keepdims=True)
        y = o * lax.rsqrt(ms + EPS) * sw * (1.0 - lam_init) * gate_ref[rows, :]
        y_ref[rows, :] = y.astype(y_ref.dtype)

    meta = pl.ds(0, N_META)

    @pl.when(i == 0)
    def _():
        qp = _split_components(q_ref[meta, :])
        carry = _attend(qp, k_ref[meta, :], v_ref[meta, :], init(2 * N_META), causal(N_META))
        finish(carry, N_META, meta)

    q_start = pl.multiple_of(N_META + i * tq, N_META)
    rows = pl.ds(q_start, tq)
    qp = _split_components(q_ref[rows, :])
    carry = _attend(qp, k_ref[meta, :], v_ref[meta, :], init(2 * tq))

    def body(j, carry):
        blk = pl.ds(pl.multiple_of(N_META + j * tq, N_META), tq)
        return _attend(qp, k_ref[blk, :], v_ref[blk, :], carry)

    carry = lax.fori_loop(0, i, body, carry)
    carry = _attend(qp, k_ref[rows, :], v_ref[rows, :], carry, causal(tq))
    finish(carry, tq, rows)


def _attn_prompt(aq, kbf, vbf, gatt, sw, lamv, layer, tq):
    bsz, seq, _ = aq.shape
    assert (seq - N_META) % tq == 0
    tok = pl.BlockSpec((None, seq, HEAD), lambda b, h, i: (b, 0, h))
    const = lambda b, h, i: (0, 0)
    return pl.pallas_call(
        functools.partial(_attn_prompt_kernel, tq=tq, lam_init=_lam_init(layer)),
        grid=(bsz, N_HEADS, (seq - N_META) // tq),
        in_specs=[tok, tok, tok, tok, pl.BlockSpec((1, HEAD), const),
                  pl.BlockSpec((4, ATT_DH), const)],
        out_specs=tok,
        out_shape=jax.ShapeDtypeStruct((bsz, seq, GROUP), BF16),
        compiler_params=pltpu.CompilerParams(
            dimension_semantics=("parallel", "parallel", "arbitrary"),
            vmem_limit_bytes=VMEM_LIMIT),
        name="attn_prompt",
    )(aq, kbf, vbf, gatt, sw, lamv)


def _attn_decode_kernel(pt_ref, q_ref, kn_ref, vn_ref, gate_ref, sw_ref, lamv_ref, *rest,
                        pages, lam_init):
    k_refs = rest[:pages]
    v_refs = rest[pages:2 * pages]
    y_ref = rest[2 * pages]
    q8_ref, m_ref, l_ref, acc_ref = rest[2 * pages + 1:]
    g = pl.program_id(1)
    n_comp = 2 * N_HEADS

    @pl.when(g == 0)
    def _():
        q = jnp.broadcast_to(q_ref[...].astype(F32), (n_comp, GROUP))
        lane = lax.broadcasted_iota(jnp.int32, (n_comp, GROUP), 1)
        lo = lax.broadcasted_iota(jnp.int32, (n_comp, GROUP), 0) * ATT_DH
        q8 = jnp.where((lane >= lo) & (lane < lo + ATT_DH), q, 0.0).astype(BF16)
        q8_ref[...] = q8
        kn = jnp.broadcast_to(kn_ref[...].astype(BF16), (n_comp, GROUP))
        s_new = lax.dot_general(kn, q8, NT_DIMS, preferred_element_type=F32)
        m_ref[...] = s_new[0:1]
        l_ref[...] = jnp.ones_like(l_ref)
        row0 = lax.broadcasted_iota(jnp.int32, (8, HEAD), 0) == 0
        for j in range(n_comp):
            h = j // 2
            vn = jnp.broadcast_to(vn_ref[:, h * HEAD:(h + 1) * HEAD], (8, HEAD))
            acc_ref[j] = jnp.where(row0, vn, 0.0)

    q8 = q8_ref[...]
    scores = [lax.dot_general(k_refs[p][...].astype(BF16), q8, NT_DIMS,
                              preferred_element_type=F32) for p in range(pages)]
    m_old = m_ref[...]
    m_new = m_old
    for s in scores:
        m_new = jnp.maximum(m_new, jnp.max(s, axis=0, keepdims=True))
    alpha = jnp.exp(m_old - m_new)
    probs = [jnp.exp(s - m_new) for s in scores]
    l_new = alpha * l_ref[...]
    for p in probs:
        l_new = l_new + jnp.sum(p, axis=0, keepdims=True)
    m_ref[...] = m_new
    l_ref[...] = l_new
    for j in range(n_comp):
        h = j // 2
        part = acc_ref[j] * alpha[:, j:j + 1]
        for p in range(pages):
            pv = probs[p][:, j:j + 1] * v_refs[p][:, h * HEAD:(h + 1) * HEAD]
            part = part + jnp.sum(pv.reshape(PAGE // 8, 8, HEAD), axis=0)
        acc_ref[j] = part

    @pl.when(g == pl.num_programs(1) - 1)
    def _():
        lam = _lam(lamv_ref[...], lam_init)
        l = l_ref[...]
        sw = sw_ref[...]
        for h in range(N_HEADS):
            o1 = jnp.sum(acc_ref[2 * h], axis=0, keepdims=True) / l[:, 2 * h:2 * h + 1]
            o2 = jnp.sum(acc_ref[2 * h + 1], axis=0, keepdims=True) / l[:, 2 * h + 1:2 * h + 2]
            o = o1 - lam * o2
            ms = jnp.mean(o * o, axis=-1, keepdims=True)
            cols = slice(h * HEAD, (h + 1) * HEAD)
            y = o * lax.rsqrt(ms + EPS) * sw * (1.0 - lam_init) * gate_ref[:, cols]
            y_ref[:, cols] = y.astype(y_ref.dtype)


def _attn_decode(page_table, aq, kn, vn, gatt, sw, lamv, cache_k, cache_v, layer, pages):
    n_seq, n_pages = page_table.shape
    assert n_pages % pages == 0
    tok = pl.BlockSpec((None, 1, GROUP), lambda b, g, pt: (b, 0, 0))
    const = lambda b, g, pt: (0, 0)

    def page_spec(p):
        return pl.BlockSpec((None, None, PAGE, GROUP),
                            lambda b, g, pt: (layer, pt[b, g * pages + p], 0, 0))

    n_comp = 2 * N_HEADS
    grid_spec = pltpu.PrefetchScalarGridSpec(
        num_scalar_prefetch=1,
        grid=(n_seq, n_pages // pages),
        in_specs=[tok, tok, tok, tok, pl.BlockSpec((1, HEAD), const),
                  pl.BlockSpec((4, ATT_DH), const)]
                 + [page_spec(p) for p in range(pages)] * 2,
        out_specs=tok,
        scratch_shapes=[pltpu.VMEM((n_comp, GROUP), BF16), pltpu.VMEM((1, n_comp), F32),
                        pltpu.VMEM((1, n_comp), F32), pltpu.VMEM((n_comp, 8, HEAD), F32)],
    )
    r3 = lambda a: a.reshape(n_seq, 1, GROUP)
    return pl.pallas_call(
        functools.partial(_attn_decode_kernel, pages=pages, lam_init=_lam_init(layer)),
        grid_spec=grid_spec,
        out_shape=jax.ShapeDtypeStruct((n_seq, 1, GROUP), BF16),
        compiler_params=pltpu.CompilerParams(
            dimension_semantics=("parallel", "arbitrary"), vmem_limit_bytes=VMEM_LIMIT),
        name="attn_decode",
    )(page_table, r3(aq), r3(kn), r3(vn), r3(gatt), sw, lamv,
      *([cache_k] * pages), *([cache_v] * pages)).reshape(n_seq, GROUP)


def _outproj_kernel(h_ref, yr_ref, ya_ref, w_ref, fw_ref, o_ref, *, final):
    h = (h_ref[...]
         + jnp.dot(yr_ref[...], w_ref[:GROUP, :], preferred_element_type=F32)
         + jnp.dot(ya_ref[...], w_ref[GROUP:, :], preferred_element_type=F32))
    if final:
        ms = jnp.mean(h * h, axis=-1, keepdims=True)
        h = h * lax.rsqrt(ms + EPS) * fw_ref[...]
    o_ref[...] = h


def _outproj(h, y_rec, y_att, w, fw, final, tm):
    m = h.shape[0]
    row = lambda i: (i, 0)
    const = lambda i: (0, 0)
    return pl.pallas_call(
        functools.partial(_outproj_kernel, final=final),
        grid=(m // tm,),
        in_specs=[pl.BlockSpec((tm, D_MODEL), row), pl.BlockSpec((tm, GROUP), row),
                  pl.BlockSpec((tm, GROUP), row), pl.BlockSpec((2 * GROUP, D_MODEL), const),
                  pl.BlockSpec((1, D_MODEL), const)],
        out_specs=pl.BlockSpec((tm, D_MODEL), row),
        out_shape=jax.ShapeDtypeStruct((m, D_MODEL), F32),
        compiler_params=pltpu.CompilerParams(
            dimension_semantics=("parallel",), vmem_limit_bytes=VMEM_LIMIT),
        name=f"outproj_m{m}",
    )(h, y_rec, y_att, w, fw)


PROMPT_TM = 384
PROMPT_TQ = 256
DECODE_PAGES = 8


def kernel(x_prompt, x_sample, cache_k, cache_v, state_hgrn, page_table, meta_tokens, norm_w, w_in,
           w_out, lb_logits, hgrn_norm_w, subln_w, lam_q1, lam_k1, lam_q2, lam_k2, final_norm_w):
    depth = w_in.shape[0]
    bp, seq_x, _ = x_prompt.shape
    seq = seq_x + N_META
    n_dec = x_sample.shape[0]
    n_phys = cache_k.shape[1]

    p = jax.nn.softmax(lb_logits.astype(F32), axis=0)
    c = jnp.cumsum(p, axis=0)
    lb = c - c[0:1]
    loglb, log1mlb, omlb = jnp.log(lb), jnp.log1p(-lb), 1.0 - lb
    w_in_bf = w_in.astype(BF16)
    w_out_bf = w_out.astype(BF16)
    lamv = jnp.stack([lam_q1, lam_k1, lam_q2, lam_k2], axis=1).astype(F32)
    fw = final_norm_w.reshape(1, D_MODEL)
    ck = cache_k.reshape(depth, n_phys, PAGE, GROUP)
    cv = cache_v.reshape(depth, n_phys, PAGE, GROUP)

    meta = jnp.broadcast_to(meta_tokens[None].astype(x_prompt.dtype), (bp, N_META, D_MODEL))
    hp = jnp.concatenate([meta, x_prompt], axis=1).reshape(bp * seq, D_MODEL)
    hs = x_sample.reshape(n_dec, D_MODEL)
    zero_state = jnp.zeros((bp, N_HEADS, HEAD, HEAD), F32)

    kv_p = None
    kv_s = None
    sp, ss = [], []
    for l in range(depth):
        row = lambda a: a[l].reshape(1, -1)
        par = (row(norm_w), w_in_bf[l], row(loglb), row(log1mlb), row(omlb))
        final = l == depth - 1

        rq, logf, kin, ri, grec, aq, k_all, v_all, kbf, vbf, gatt = _inproj(
            hp, *par, l, depth, kv_p, PROMPT_TM)
        kv_p = (k_all, v_all)
        b3 = lambda a: a.reshape(bp, seq, GROUP)
        y_rec, s_p = _hgrn(b3(rq), b3(logf), b3(kin), b3(ri), b3(grec), zero_state,
                           row(hgrn_norm_w), "hgrn_prompt")
        y_att = _attn_prompt(b3(aq), b3(kbf), b3(vbf), b3(gatt), row(subln_w), lamv[l], l,
                             PROMPT_TQ)
        hp = _outproj(hp, y_rec.reshape(bp * seq, GROUP), y_att.reshape(bp * seq, GROUP),
                      w_out_bf[l], fw, final, PROMPT_TM)
        sp.append(s_p)

        rq, logf, kin, ri, grec, aq, k_all, v_all, kbf, vbf, gatt = _inproj(
            hs, *par, l, depth, kv_s, n_dec)
        kv_s = (k_all, v_all)
        pad = lambda a: jnp.pad(a.reshape(n_dec, 1, GROUP), ((0, 0), (0, SUB - 1), (0, 0)))
        y_rec, s_s = _hgrn(pad(rq), pad(logf), pad(kin), pad(ri), pad(grec), state_hgrn[l],
                           row(hgrn_norm_w), "hgrn_decode")
        y_att = _attn_decode(page_table, aq, k_all[l], v_all[l], gatt, row(subln_w), lamv[l],
                             ck, cv, l, DECODE_PAGES)
        hs = _outproj(hs, y_rec[:, 0, :], y_att, w_out_bf[l], fw, final, n_dec)
        ss.append(s_s)

    n_ah = GROUP // HEAD
    y_prompt = hp.reshape(bp, seq, D_MODEL)[:, N_META:]
    y_sample = hs.reshape(n_dec, 1, D_MODEL)
    return (y_prompt, y_sample,
            kv_p[0].reshape(depth, bp, seq, n_ah, HEAD), kv_p[1].reshape(depth, bp, seq, n_ah, HEAD),
            jnp.stack(sp),
            kv_s[0].reshape(depth, n_dec, 1, n_ah, HEAD), kv_s[1].reshape(depth, n_dec, 1, n_ah, HEAD),
            jnp.stack(ss))
```

```python
import functools
import math

import jax
import jax.numpy as jnp
from jax import lax
from jax.experimental import pallas as pl
from jax.experimental.pallas import tpu as pltpu

F32 = jnp.float32
BF16 = jnp.bfloat16

D_MODEL = 1024
N_META = 16
GROUP = 512
HEAD = 128
N_HEADS = 4
ATT_DH = 64
PAGE = 128
CHUNK = 64
SUB = 16
EPS = 1e-6
VMEM_LIMIT = 52 * 1024 * 1024

NT_DIMS = (((1,), (1,)), ((), ()))
TN_DIMS = (((0,), (0,)), ((), ()))


def _lam_init(layer):
    return 0.8 - 0.6 * math.exp(-0.3 * layer)


def _sigmoid(x):
    return 1.0 / (1.0 + jnp.exp(-x))


def _inproj_kernel(*refs, aliased):
    if aliased:
        refs = refs[:6] + refs[8:]
    (x_ref, nw_ref, w_ref, loglb_ref, log1mlb_ref, omlb_ref,
     rq_ref, logf_ref, kin_ref, ri_ref, grec_ref, aq_ref, k_ref, v_ref, kbf_ref, vbf_ref,
     gatt_ref) = refs
    x = x_ref[...]
    ms = jnp.mean(x * x, axis=-1, keepdims=True)
    xn = (x * lax.rsqrt(ms + EPS) * nw_ref[...]).astype(BF16)

    def proj(g):
        return jnp.dot(xn, w_ref[:, g * GROUP:(g + 1) * GROUP], preferred_element_type=F32)

    rq_ref[...] = proj(0).astype(BF16)
    z = proj(1)
    log_sig = jnp.minimum(z, 0.0) - jnp.log1p(jnp.exp(-jnp.abs(z)))
    a = loglb_ref[...]
    b = log1mlb_ref[...] + log_sig
    logf_ref[...] = jnp.maximum(a, b) + jnp.log1p(jnp.exp(-jnp.abs(a - b)))
    kin_ref[...] = omlb_ref[...] * _sigmoid(-z)
    ri_ref[...] = proj(2).astype(BF16)
    g = proj(3)
    grec_ref[...] = g * _sigmoid(g)
    aq_ref[...] = (proj(4) * (ATT_DH ** -0.5)).astype(BF16)
    k = proj(5)
    k_ref[...] = k
    kbf_ref[...] = k.astype(BF16)
    v = proj(6)
    v_ref[...] = v
    vbf_ref[...] = v.astype(BF16)
    g = proj(7)
    gatt_ref[...] = g * _sigmoid(g)


def _inproj(h, nw, w, loglb, log1mlb, omlb, layer, depth, kv_bufs, tm):
    m = h.shape[0]
    grid = (m // tm,)
    row = lambda i: (i, 0)
    const = lambda i: (0, 0)
    in_specs = [
        pl.BlockSpec((tm, D_MODEL), row),
        pl.BlockSpec((1, D_MODEL), const),
        pl.BlockSpec((D_MODEL, 8 * GROUP), const),
        pl.BlockSpec((1, GROUP), const),
        pl.BlockSpec((1, GROUP), const),
        pl.BlockSpec((1, GROUP), const),
    ]
    args = [h, nw, w, loglb, log1mlb, omlb]
    aliases = {}
    if kv_bufs is not None:
        in_specs += [pl.BlockSpec(memory_space=pl.ANY)] * 2
        args += list(kv_bufs)
        aliases = {6: 6, 7: 7}
    tile = lambda dt: jax.ShapeDtypeStruct((m, GROUP), dt)
    stacked = jax.ShapeDtypeStruct((depth, m, GROUP), F32)
    out_shape = [tile(BF16), tile(F32), tile(F32), tile(BF16), tile(F32), tile(BF16),
                 stacked, stacked, tile(BF16), tile(BF16), tile(F32)]
    tile_spec = pl.BlockSpec((tm, GROUP), row)
    stacked_spec = pl.BlockSpec((None, tm, GROUP), lambda i: (layer, i, 0))
    out_specs = [tile_spec] * 6 + [stacked_spec] * 2 + [tile_spec] * 3
    return pl.pallas_call(
        functools.partial(_inproj_kernel, aliased=kv_bufs is not None),
        grid=grid, in_specs=in_specs, out_specs=out_specs, out_shape=out_shape,
        input_output_aliases=aliases,
        compiler_params=pltpu.CompilerParams(
            dimension_semantics=("parallel",), vmem_limit_bytes=VMEM_LIMIT),
        name=f"inproj_m{m}",
    )(*args)


def _hgrn_kernel(q_ref, g_ref, k_ref, v_ref, gate_ref, s0_ref, nw_ref, y_ref, sout_ref, *, seq):
    first = seq % CHUNK
    n_full = seq // CHUNK
    nw = nw_ref[...]

    def chunk(start, c, st):
        sl = pl.ds(start, c)
        g = g_ref[sl, :]
        q = q_ref[sl, :].astype(F32)
        k = k_ref[sl, :]
        v = v_ref[sl, :]
        vf = v.astype(F32)
        r_i = lax.broadcasted_iota(jnp.int32, (c, c), 0)
        c_i = lax.broadcasted_iota(jnp.int32, (c, c), 1)
        tril = (r_i >= c_i).astype(F32)
        b = jnp.dot(tril, g, precision=lax.Precision.HIGHEST, preferred_element_type=F32)
        o = lax.dot_general((q * jnp.exp(b)).astype(BF16), st.astype(BF16), NT_DIMS,
                            preferred_element_type=F32)
        t_i = lax.broadcasted_iota(jnp.int32, (SUB, 1), 0)
        parts = []
        for s in range(c // SUB):
            lo = s * SUB
            b_s = b[lo:lo + SUB]
            q_s = q[lo:lo + SUB]
            k_s = k[lo:lo + SUB]
            v_s = vf[lo:lo + SUB]
            o_s = o[lo:lo + SUB]
            if s > 0:
                edge = b[lo - 1:lo]
                qh = (q_s * jnp.exp(b_s - edge)).astype(BF16)
                kh = (k[:lo] * jnp.exp(edge - b[:lo])).astype(BF16)
                a = lax.dot_general(qh, kh, NT_DIMS, preferred_element_type=F32)
                o_s = o_s + jnp.dot(a.astype(BF16), v[:lo], preferred_element_type=F32)
            for j in range(SUB):
                e = jnp.exp(jnp.minimum(b_s - b_s[j:j + 1], 0.0))
                col = jnp.sum(q_s * e * k_s[j:j + 1], axis=-1, keepdims=True)
                col = jnp.where(t_i >= j, col, 0.0)
                o_s = o_s + col * v_s[j:j + 1]
            parts.append(o_s)
        o = parts[0] if len(parts) == 1 else jnp.concatenate(parts, axis=0)
        ms = jnp.mean(o * o, axis=-1, keepdims=True)
        y = o * lax.rsqrt(ms + EPS) * nw * gate_ref[sl, :]
        y_ref[sl, :] = y.astype(y_ref.dtype)
        b_last = b[c - 1:c]
        kd = (k * jnp.exp(b_last - b)).astype(BF16)
        upd = lax.dot_general(v, kd, TN_DIMS, preferred_element_type=F32)
        return st * jnp.exp(b_last) + upd

    st = s0_ref[...].T
    if first:
        st = chunk(0, first, st)
    if n_full:
        st = lax.fori_loop(
            0, n_full,
            lambda j, s: chunk(pl.multiple_of(first + j * CHUNK, SUB), CHUNK, s), st)
    sout_ref[...] = st.T


def _hgrn(rq, logf, kin, ri, grec, s0, nw, name):
    bsz, seq, _ = rq.shape
    assert seq % SUB == 0
    tok = pl.BlockSpec((None, seq, HEAD), lambda b, h: (b, 0, h))
    state = pl.BlockSpec((None, None, HEAD, HEAD), lambda b, h: (b, h, 0, 0))
    return pl.pallas_call(
        functools.partial(_hgrn_kernel, seq=seq),
        grid=(bsz, N_HEADS),
        in_specs=[tok, tok, tok, tok, tok, state, pl.BlockSpec((1, HEAD), lambda b, h: (0, 0))],
        out_specs=[tok, state],
        out_shape=[jax.ShapeDtypeStruct((bsz, seq, GROUP), BF16),
                   jax.ShapeDtypeStruct(s0.shape, F32)],
        compiler_params=pltpu.CompilerParams(
            dimension_semantics=("parallel", "parallel"), vmem_limit_bytes=VMEM_LIMIT),
        name=name,
    )(rq, logf, kin, ri, grec, s0, nw)


def _lam(lamv, lam_init):
    return (jnp.exp(jnp.sum(lamv[0:1] * lamv[1:2], axis=-1, keepdims=True))
            - jnp.exp(jnp.sum(lamv[2:3] * lamv[3:4], axis=-1, keepdims=True)) + lam_init)


def _split_components(q):
    lane = lax.broadcasted_iota(jnp.int32, q.shape, 1)
    qf = q.astype(F32)
    return jnp.concatenate([jnp.where(lane < ATT_DH, qf, 0.0), jnp.where(lane >= ATT_DH, qf, 0.0)],
                           axis=0).astype(BF16)


def _attend(qp, kb, vb, carry, mask=None):
    m, l, acc = carry
    s = lax.dot_general(qp, kb, NT_DIMS, preferred_element_type=F32)
    if mask is not None:
        s = jnp.where(mask, s, -jnp.inf)
    m_new = jnp.maximum(m, jnp.max(s, axis=-1, keepdims=True))
    alpha = jnp.exp(m - m_new)
    p = jnp.exp(s - m_new)
    l = alpha * l + jnp.sum(p, axis=-1, keepdims=True)
    acc = alpha * acc + jnp.dot(p.astype(BF16), vb, preferred_element_type=F32)
    return m_new, l, acc


def _attn_prompt_kernel(q_ref, k_ref, v_ref, gate_ref, sw_ref, lamv_ref, y_ref, *, tq, lam_init):
    i = pl.program_id(2)
    lam = _lam(lamv_ref[...], lam_init)
    sw = sw_ref[...]

    def init(rows):
        return (jnp.full((rows, 1), -jnp.inf, F32), jnp.zeros((rows, 1), F32),
                jnp.zeros((rows, HEAD), F32))

    def causal(t):
        r_i = lax.broadcasted_iota(jnp.int32, (2 * t, t), 0)
        c_i = lax.broadcasted_iota(jnp.int32, (2 * t, t), 1)
        return c_i <= jnp.where(r_i >= t, r_i - t, r_i)

    def finish(carry, t, rows):
        _, l, acc = carry
        o = acc[:t] / l[:t] - lam * (acc[t:] / l[t:])
        ms = jnp.mean(o * o, axis=-1, keepdims=True)
        y = o * lax.rsqrt(ms + EPS) * sw * (1.0 - lam_init) * gate_ref[rows, :]
        y_ref[rows, :] = y.astype(y_ref.dtype)

    meta = pl.ds(0, N_META)

    @pl.when(i == 0)
    def _():
        qp = _split_components(q_ref[meta, :])
        carry = _attend(qp, k_ref[meta, :], v_ref[meta, :], init(2 * N_META), causal(N_META))
        finish(carry, N_META, meta)

    q_start = pl.multiple_of(N_META + i * tq, N_META)
    rows = pl.ds(q_start, tq)
    qp = _split_components(q_ref[rows, :])
    carry = _attend(qp, k_ref[meta, :], v_ref[meta, :], init(2 * tq))

    def body(j, carry):
        blk = pl.ds(pl.multiple_of(N_META + j * tq, N_META), tq)
        return _attend(qp, k_ref[blk, :], v_ref[blk, :], carry)

    carry = lax.fori_loop(0, i, body, carry)
    carry = _attend(qp, k_ref[rows, :], v_ref[rows, :], carry, causal(tq))
    finish(carry, tq, rows)


def _attn_prompt(aq, kbf, vbf, gatt, sw, lamv, layer, tq):
    bsz, seq, _ = aq.shape
    assert (seq - N_META) % tq == 0
    tok = pl.BlockSpec((None, seq, HEAD), lambda b, h, i: (b, 0, h))
    const = lambda b, h, i: (0, 0)
    return pl.pallas_call(
        functools.partial(_attn_prompt_kernel, tq=tq, lam_init=_lam_init(layer)),
        grid=(bsz, N_HEADS, (seq - N_META) // tq),
        in_specs=[tok, tok, tok, tok, pl.BlockSpec((1, HEAD), const),
                  pl.BlockSpec((4, ATT_DH), const)],
        out_specs=tok,
        out_shape=jax.ShapeDtypeStruct((bsz, seq, GROUP), BF16),
        compiler_params=pltpu.CompilerParams(
            dimension_semantics=("parallel", "parallel", "arbitrary"),
            vmem_limit_bytes=VMEM_LIMIT),
        name="attn_prompt",
    )(aq, kbf, vbf, gatt, sw, lamv)


def _attn_decode_kernel(pt_ref, q_ref, kn_ref, vn_ref, gate_ref, sw_ref, lamv_ref, *rest,
                        pages, lam_init):
    k_refs = rest[:pages]
    v_refs = rest[pages:2 * pages]
    y_ref = rest[2 * pages]
    q8_ref, m_ref, l_ref, acc_ref = rest[2 * pages + 1:]
    g = pl.program_id(1)
    n_comp = 2 * N_HEADS

    @pl.when(g == 0)
    def _():
        q = jnp.broadcast_to(q_ref[...].astype(F32), (n_comp, GROUP))
        lane = lax.broadcasted_iota(jnp.int32, (n_comp, GROUP), 1)
        lo = lax.broadcasted_iota(jnp.int32, (n_comp, GROUP), 0) * ATT_DH
        q8 = jnp.where((lane >= lo) & (lane < lo + ATT_DH), q, 0.0).astype(BF16)
        q8_ref[...] = q8
        kn = jnp.broadcast_to(kn_ref[...].astype(BF16), (n_comp, GROUP))
        s_new = lax.dot_general(kn, q8, NT_DIMS, preferred_element_type=F32)
        m_ref[...] = s_new[0:1]
        l_ref[...] = jnp.ones_like(l_ref)
        row0 = lax.broadcasted_iota(jnp.int32, (8, HEAD), 0) == 0
        for j in range(n_comp):
            h = j // 2
            vn = jnp.broadcast_to(vn_ref[:, h * HEAD:(h + 1) * HEAD], (8, HEAD))
            acc_ref[j] = jnp.where(row0, vn, 0.0)

    def head_rows(ref, h):
        return ref[pl.ds(h, PAGE, stride=N_HEADS), :]

    q8 = q8_ref[...]
    scores = []
    for p in range(pages):
        kb = jnp.concatenate([head_rows(k_refs[p], h) for h in range(N_HEADS)], axis=1)
        scores.append(lax.dot_general(kb.astype(BF16), q8, NT_DIMS,
                                      preferred_element_type=F32))
    m_old = m_ref[...]
    m_new = m_old
    for s in scores:
        m_new = jnp.maximum(m_new, jnp.max(s, axis=0, keepdims=True))
    alpha = jnp.exp(m_old - m_new)
    probs = [jnp.exp(s - m_new) for s in scores]
    l_new = alpha * l_ref[...]
    for p in probs:
        l_new = l_new + jnp.sum(p, axis=0, keepdims=True)
    m_ref[...] = m_new
    l_ref[...] = l_new
    p_all = jnp.concatenate(probs, axis=0).astype(BF16)
    sel_row = lax.broadcasted_iota(jnp.int32, (n_comp, 2 * HEAD), 0)
    sel_col = lax.broadcasted_iota(jnp.int32, (n_comp, 2 * HEAD), 1)
    for h in range(N_HEADS):
        sel = jnp.where(sel_row == jnp.where(sel_col < HEAD, 2 * h, 2 * h + 1), 1.0, 0.0)
        pb = jnp.dot(p_all, sel.astype(BF16), preferred_element_type=F32)
        part = [acc_ref[2 * h + c] * alpha[:, 2 * h + c:2 * h + c + 1] for c in range(2)]
        for p in range(pages):
            vh = head_rows(v_refs[p], h)
            for c in range(2):
                pv = pb[p * PAGE:(p + 1) * PAGE, c * HEAD:(c + 1) * HEAD] * vh
                part[c] = part[c] + jnp.sum(pv.reshape(PAGE // 8, 8, HEAD), axis=0)
        for c in range(2):
            acc_ref[2 * h + c] = part[c]

    @pl.when(g == pl.num_programs(1) - 1)
    def _():
        lam = _lam(lamv_ref[...], lam_init)
        l = l_ref[...]
        sw = sw_ref[...]
        for h in range(N_HEADS):
            o1 = jnp.sum(acc_ref[2 * h], axis=0, keepdims=True) / l[:, 2 * h:2 * h + 1]
            o2 = jnp.sum(acc_ref[2 * h + 1], axis=0, keepdims=True) / l[:, 2 * h + 1:2 * h + 2]
            o = o1 - lam * o2
            ms = jnp.mean(o * o, axis=-1, keepdims=True)
            cols = slice(h * HEAD, (h + 1) * HEAD)
            y = o * lax.rsqrt(ms + EPS) * sw * (1.0 - lam_init) * gate_ref[:, cols]
            y_ref[:, cols] = y.astype(y_ref.dtype)


def _attn_decode(page_table, aq, kn, vn, gatt, sw, lamv, cache_k, cache_v, layer, pages):
    n_seq, n_pages = page_table.shape
    assert n_pages % pages == 0
    tok = pl.BlockSpec((None, 1, GROUP), lambda b, g, pt: (b, 0, 0))
    const = lambda b, g, pt: (0, 0)

    def page_spec(p):
        return pl.BlockSpec((None, None, PAGE * N_HEADS, HEAD),
                            lambda b, g, pt: (layer, pt[b, g * pages + p], 0, 0))

    n_comp = 2 * N_HEADS
    grid_spec = pltpu.PrefetchScalarGridSpec(
        num_scalar_prefetch=1,
        grid=(n_seq, n_pages // pages),
        in_specs=[tok, tok, tok, tok, pl.BlockSpec((1, HEAD), const),
                  pl.BlockSpec((4, ATT_DH), const)]
                 + [page_spec(p) for p in range(pages)] * 2,
        out_specs=tok,
        scratch_shapes=[pltpu.VMEM((n_comp, GROUP), BF16), pltpu.VMEM((1, n_comp), F32),
                        pltpu.VMEM((1, n_comp), F32), pltpu.VMEM((n_comp, 8, HEAD), F32)],
    )
    r3 = lambda a: a.reshape(n_seq, 1, GROUP)
    return pl.pallas_call(
        functools.partial(_attn_decode_kernel, pages=pages, lam_init=_lam_init(layer)),
        grid_spec=grid_spec,
        out_shape=jax.ShapeDtypeStruct((n_seq, 1, GROUP), BF16),
        compiler_params=pltpu.CompilerParams(
            dimension_semantics=("parallel", "arbitrary"), vmem_limit_bytes=VMEM_LIMIT),
        name="attn_decode",
    )(page_table, r3(aq), r3(kn), r3(vn), r3(gatt), sw, lamv,
      *([cache_k] * pages), *([cache_v] * pages)).reshape(n_seq, GROUP)


def _outproj_kernel(h_ref, yr_ref, ya_ref, w_ref, fw_ref, o_ref, *, final):
    h = (h_ref[...]
         + jnp.dot(yr_ref[...], w_ref[:GROUP, :], preferred_element_type=F32)
         + jnp.dot(ya_ref[...], w_ref[GROUP:, :], preferred_element_type=F32))
    if final:
        ms = jnp.mean(h * h, axis=-1, keepdims=True)
        h = h * lax.rsqrt(ms + EPS) * fw_ref[...]
    o_ref[...] = h


def _outproj(h, y_rec, y_att, w, fw, final, tm):
    m = h.shape[0]
    row = lambda i: (i, 0)
    const = lambda i: (0, 0)
    return pl.pallas_call(
        functools.partial(_outproj_kernel, final=final),
        grid=(m // tm,),
        in_specs=[pl.BlockSpec((tm, D_MODEL), row), pl.BlockSpec((tm, GROUP), row),
                  pl.BlockSpec((tm, GROUP), row), pl.BlockSpec((2 * GROUP, D_MODEL), const),
                  pl.BlockSpec((1, D_MODEL), const)],
        out_specs=pl.BlockSpec((tm, D_MODEL), row),
        out_shape=jax.ShapeDtypeStruct((m, D_MODEL), F32),
        compiler_params=pltpu.CompilerParams(
            dimension_semantics=("parallel",), vmem_limit_bytes=VMEM_LIMIT),
        name=f"outproj_m{m}",
    )(h, y_rec, y_att, w, fw)


PROMPT_TM = 384
PROMPT_TQ = 256
DECODE_PAGES = 8


def kernel(x_prompt, x_sample, cache_k, cache_v, state_hgrn, page_table, meta_tokens, norm_w, w_in,
           w_out, lb_logits, hgrn_norm_w, subln_w, lam_q1, lam_k1, lam_q2, lam_k2, final_norm_w):
    depth = w_in.shape[0]
    bp, seq_x, _ = x_prompt.shape
    seq = seq_x + N_META
    n_dec = x_sample.shape[0]
    n_phys = cache_k.shape[1]

    p = jax.nn.softmax(lb_logits.astype(F32), axis=0)
    c = jnp.cumsum(p, axis=0)
    lb = c - c[0:1]
    loglb, log1mlb, omlb = jnp.log(lb), jnp.log1p(-lb), 1.0 - lb
    w_in_bf = w_in.astype(BF16)
    w_out_bf = w_out.astype(BF16)
    lamv = jnp.stack([lam_q1, lam_k1, lam_q2, lam_k2], axis=1).astype(F32)
    fw = final_norm_w.reshape(1, D_MODEL)
    ck = cache_k.reshape(depth, n_phys, PAGE * N_HEADS, HEAD)
    cv = cache_v.reshape(depth, n_phys, PAGE * N_HEADS, HEAD)

    meta = jnp.broadcast_to(meta_tokens[None].astype(x_prompt.dtype), (bp, N_META, D_MODEL))
    hp = jnp.concatenate([meta, x_prompt], axis=1).reshape(bp * seq, D_MODEL)
    hs = x_sample.reshape(n_dec, D_MODEL)
    zero_state = jnp.zeros((bp, N_HEADS, HEAD, HEAD), F32)

    kv_p = None
    kv_s = None
    sp, ss = [], []
    for l in range(depth):
        row = lambda a: a[l].reshape(1, -1)
        par = (row(norm_w), w_in_bf[l], row(loglb), row(log1mlb), row(omlb))
        final = l == depth - 1

        rq, logf, kin, ri, grec, aq, k_all, v_all, kbf, vbf, gatt = _inproj(
            hp, *par, l, depth, kv_p, PROMPT_TM)
        kv_p = (k_all, v_all)
        b3 = lambda a: a.reshape(bp, seq, GROUP)
        y_rec, s_p = _hgrn(b3(rq), b3(logf), b3(kin), b3(ri), b3(grec), zero_state,
                           row(hgrn_norm_w), "hgrn_prompt")
        y_att = _attn_prompt(b3(aq), b3(kbf), b3(vbf), b3(gatt), row(subln_w), lamv[l], l,
                             PROMPT_TQ)
        hp = _outproj(hp, y_rec.reshape(bp * seq, GROUP), y_att.reshape(bp * seq, GROUP),
                      w_out_bf[l], fw, final, PROMPT_TM)
        sp.append(s_p)

        rq, logf, kin, ri, grec, aq, k_all, v_all, kbf, vbf, gatt = _inproj(
            hs, *par, l, depth, kv_s, n_dec)
        kv_s = (k_all, v_all)
        pad = lambda a: jnp.pad(a.reshape(n_dec, 1, GROUP), ((0, 0), (0, SUB - 1), (0, 0)))
        y_rec, s_s = _hgrn(pad(rq), pad(logf), pad(kin), pad(ri), pad(grec), state_hgrn[l],
                           row(hgrn_norm_w), "hgrn_decode")
        y_att = _attn_decode(page_table, aq, k_all[l], v_all[l], gatt, row(subln_w), lamv[l],
                             ck, cv, l, DECODE_PAGES)
        hs = _outproj(hs, y_rec[:, 0, :], y_att, w_out_bf[l], fw, final, n_dec)
        ss.append(s_s)

    n_ah = GROUP // HEAD
    y_prompt = hp.reshape(bp, seq, D_MODEL)[:, N_META:]
    y_sample = hs.reshape(n_dec, 1, D_MODEL)
    return (y_prompt, y_sample,
            kv_p[0].reshape(depth, bp, seq, n_ah, HEAD), kv_p[1].reshape(depth, bp, seq, n_ah, HEAD),
            jnp.stack(sp),
            kv_s[0].reshape(depth, n_dec, 1, n_ah, HEAD), kv_s[1].reshape(depth, n_dec, 1, n_ah, HEAD),
            jnp.stack(ss))
```

```python
import functools
import math

import jax
import jax.numpy as jnp
from jax import lax
from jax.experimental import pallas as pl
from jax.experimental.pallas import tpu as pltpu

F32 = jnp.float32
BF16 = jnp.bfloat16

D_MODEL = 1024
N_META = 16
GROUP = 512
HEAD = 128
N_HEADS = 4
ATT_DH = 64
PAGE = 128
CHUNK = 64
SUB = 16
EPS = 1e-6
LOG2E = 1.4426950408889634
VMEM_LIMIT = 52 * 1024 * 1024

NT_DIMS = (((1,), (1,)), ((), ()))
TN_DIMS = (((0,), (0,)), ((), ()))


def _lam_init(layer):
    return 0.8 - 0.6 * math.exp(-0.3 * layer)


def _sigmoid(x):
    return 1.0 / (1.0 + jnp.exp(-x))


def _inproj_kernel(*refs, aliased):
    if aliased:
        refs = refs[:6] + refs[8:]
    (x_ref, nw_ref, w_ref, loglb_ref, log1mlb_ref, omlb_ref,
     rq_ref, logf_ref, kin_ref, ri_ref, grec_ref, aq_ref, k_ref, v_ref, kbf_ref, vbf_ref,
     gatt_ref) = refs
    x = x_ref[...]
    ms = jnp.mean(x * x, axis=-1, keepdims=True)
    xn = (x * lax.rsqrt(ms + EPS) * nw_ref[...]).astype(BF16)

    def proj(g):
        return jnp.dot(xn, w_ref[:, g * GROUP:(g + 1) * GROUP], preferred_element_type=F32)

    rq_ref[...] = proj(0).astype(BF16)
    z = proj(1)
    log_sig = jnp.minimum(z, 0.0) - jnp.log1p(jnp.exp(-jnp.abs(z)))
    a = loglb_ref[...]
    b = log1mlb_ref[...] + log_sig
    logf_ref[...] = jnp.maximum(a, b) + jnp.log1p(jnp.exp(-jnp.abs(a - b)))
    kin_ref[...] = omlb_ref[...] * _sigmoid(-z)
    ri_ref[...] = proj(2).astype(BF16)
    g = proj(3)
    grec_ref[...] = g * _sigmoid(g)
    aq_ref[...] = (proj(4) * (ATT_DH ** -0.5)).astype(BF16)
    tm = x.shape[0]
    k = proj(5)
    v = proj(6)
    for h in range(N_HEADS):
        k_ref[pl.ds(h, tm, stride=N_HEADS), :] = k[:, h * HEAD:(h + 1) * HEAD]
        v_ref[pl.ds(h, tm, stride=N_HEADS), :] = v[:, h * HEAD:(h + 1) * HEAD]
    kbf_ref[...] = k.astype(BF16)
    vbf_ref[...] = v.astype(BF16)
    g = proj(7)
    gatt_ref[...] = g * _sigmoid(g)


def _inproj(h, nw, w, loglb, log1mlb, omlb, layer, depth, kv_bufs, tm):
    m = h.shape[0]
    grid = (m // tm,)
    row = lambda i: (i, 0)
    const = lambda i: (0, 0)
    in_specs = [
        pl.BlockSpec((tm, D_MODEL), row),
        pl.BlockSpec((1, D_MODEL), const),
        pl.BlockSpec((D_MODEL, 8 * GROUP), const),
        pl.BlockSpec((1, GROUP), const),
        pl.BlockSpec((1, GROUP), const),
        pl.BlockSpec((1, GROUP), const),
    ]
    args = [h, nw, w, loglb, log1mlb, omlb]
    aliases = {}
    if kv_bufs is not None:
        in_specs += [pl.BlockSpec(memory_space=pl.ANY)] * 2
        args += list(kv_bufs)
        aliases = {6: 6, 7: 7}
    tile = lambda dt: jax.ShapeDtypeStruct((m, GROUP), dt)
    stacked = jax.ShapeDtypeStruct((depth, m * N_HEADS, HEAD), F32)
    out_shape = [tile(BF16), tile(F32), tile(F32), tile(BF16), tile(F32), tile(BF16),
                 stacked, stacked, tile(BF16), tile(BF16), tile(F32)]
    tile_spec = pl.BlockSpec((tm, GROUP), row)
    stacked_spec = pl.BlockSpec((None, tm * N_HEADS, HEAD), lambda i: (layer, i, 0))
    out_specs = [tile_spec] * 6 + [stacked_spec] * 2 + [tile_spec] * 3
    return pl.pallas_call(
        functools.partial(_inproj_kernel, aliased=kv_bufs is not None),
        grid=grid, in_specs=in_specs, out_specs=out_specs, out_shape=out_shape,
        input_output_aliases=aliases,
        compiler_params=pltpu.CompilerParams(
            dimension_semantics=("parallel",), vmem_limit_bytes=VMEM_LIMIT),
        name=f"inproj_m{m}",
    )(*args)


def _hgrn_kernel(q_ref, g_ref, k_ref, v_ref, gate_ref, s0_ref, nw_ref, y_ref, sout_ref,
                 st_ref, cj_ref, vf_ref, *, seq, heads):
    first = seq % CHUNK
    n_full = seq // CHUNK
    nw = nw_ref[...]
    row8 = lax.broadcasted_iota(jnp.int32, (8, 1), 0)

    def chunk(start, c):
        sl = pl.ds(start, c)
        r_i = lax.broadcasted_iota(jnp.int32, (c, c), 0)
        c_i = lax.broadcasted_iota(jnp.int32, (c, c), 1)
        tril = (r_i >= c_i).astype(F32)
        hs = range(heads)
        cols = [slice(hh * HEAD, (hh + 1) * HEAD) for hh in hs]
        b2 = [jnp.dot(tril, g_ref[sl, cols[hh]], precision=lax.Precision.HIGHEST,
                      preferred_element_type=F32) * LOG2E for hh in hs]
        q = [q_ref[sl, cols[hh]].astype(F32) for hh in hs]
        k = [k_ref[sl, cols[hh]].astype(F32) for hh in hs]
        v = [v_ref[sl, cols[hh]] for hh in hs]
        o = [lax.dot_general((q[hh] * jnp.exp2(b2[hh])).astype(BF16), st_ref[hh].astype(BF16),
                             NT_DIMS, preferred_element_type=F32) for hh in hs]
        for hh in hs:
            cj_ref[hh, 0:c, :] = b2[hh] - jnp.log2(k[hh])
            vf_ref[hh, 0:c, :] = v[hh].astype(F32)
        off = [[None] * (c // SUB) for _ in hs]
        for s in range(1, c // SUB):
            lo = s * SUB
            for hh in hs:
                edge = b2[hh][lo - 1:lo]
                qh = (q[hh][lo:lo + SUB] * jnp.exp2(b2[hh][lo:lo + SUB] - edge)).astype(BF16)
                kh = (k[hh][:lo] * jnp.exp2(edge - b2[hh][:lo])).astype(BF16)
                a = lax.dot_general(qh, kh, NT_DIMS, preferred_element_type=F32)
                off[hh][s] = jnp.dot(a.astype(BF16), v[hh][:lo], preferred_element_type=F32)
        for hh in hs:
            parts = []
            for r in range(c // 8):
                b_r = b2[hh][8 * r:8 * r + 8]
                q_r = q[hh][8 * r:8 * r + 8]
                o_r = o[hh][8 * r:8 * r + 8]
                lo = (8 * r // SUB) * SUB
                if lo > 0:
                    o_r = o_r + off[hh][lo // SUB][8 * r - lo:8 * r - lo + 8]
                for jj in range(lo, 8 * r + 8):
                    col = jnp.sum(q_r * jnp.exp2(b_r - cj_ref[hh, jj:jj + 1, :]),
                                  axis=-1, keepdims=True)
                    if jj >= 8 * r:
                        col = jnp.where(row8 >= jj - 8 * r, col, 0.0)
                    o_r = o_r + col * vf_ref[hh, jj:jj + 1, :]
                parts.append(o_r)
            oo = jnp.concatenate(parts, axis=0)
            ms = jnp.mean(oo * oo, axis=-1, keepdims=True)
            y = oo * lax.rsqrt(ms + EPS) * nw * gate_ref[sl, cols[hh]].astype(F32)
            y_ref[sl, cols[hh]] = y.astype(y_ref.dtype)
        for hh in hs:
            b_last = b2[hh][c - 1:c]
            kd = (k[hh] * jnp.exp2(b_last - b2[hh])).astype(BF16)
            upd = lax.dot_general(v[hh], kd, TN_DIMS, preferred_element_type=F32)
            st_ref[hh] = st_ref[hh] * jnp.exp2(b_last) + upd

    for hh in range(heads):
        st_ref[hh] = s0_ref[hh].T
    if first:
        chunk(0, first)
    if n_full:
        def body(j, carry):
            chunk(pl.multiple_of(first + j * CHUNK, SUB), CHUNK)
            return carry
        lax.fori_loop(0, n_full, body, 0)
    for hh in range(heads):
        sout_ref[hh] = st_ref[hh].T


def _hgrn(rq, logf, kin, ri, grec, s0, nw, heads, name):
    bsz, seq, _ = rq.shape
    assert seq % SUB == 0 and N_HEADS % heads == 0
    tok = pl.BlockSpec((None, seq, heads * HEAD), lambda b, h: (b, 0, h))
    state = pl.BlockSpec((None, heads, HEAD, HEAD), lambda b, h: (b, h, 0, 0))
    return pl.pallas_call(
        functools.partial(_hgrn_kernel, seq=seq, heads=heads),
        grid=(bsz, N_HEADS // heads),
        in_specs=[tok, tok, tok, tok, tok, state, pl.BlockSpec((1, HEAD), lambda b, h: (0, 0))],
        out_specs=[tok, state],
        out_shape=[jax.ShapeDtypeStruct((bsz, seq, GROUP), BF16),
                   jax.ShapeDtypeStruct(s0.shape, F32)],
        scratch_shapes=[pltpu.VMEM((heads, HEAD, HEAD), F32), pltpu.VMEM((heads, CHUNK, HEAD), F32),
                        pltpu.VMEM((heads, CHUNK, HEAD), F32)],
        compiler_params=pltpu.CompilerParams(
            dimension_semantics=("parallel", "parallel"), vmem_limit_bytes=VMEM_LIMIT),
        name=name,
    )(rq, logf, kin, ri, grec, s0, nw)


def _lam(lamv, lam_init):
    return (jnp.exp(jnp.sum(lamv[0:1] * lamv[1:2], axis=-1, keepdims=True))
            - jnp.exp(jnp.sum(lamv[2:3] * lamv[3:4], axis=-1, keepdims=True)) + lam_init)


def _split_components(q):
    lane = lax.broadcasted_iota(jnp.int32, q.shape, 1)
    qf = q.astype(F32)
    return jnp.concatenate([jnp.where(lane < ATT_DH, qf, 0.0), jnp.where(lane >= ATT_DH, qf, 0.0)],
                           axis=0).astype(BF16)


def _attn_prompt_kernel(q_ref, k_ref, v_ref, gate_ref, sw_ref, lamv_ref, y_ref,
                        qp_ref, m_ref, l_ref, acc_ref, *, tq, lam_init):
    i = pl.program_id(1)
    lam = _lam(lamv_ref[...], lam_init)
    sw = sw_ref[...]
    heads = range(N_HEADS)
    cols = [slice(h * HEAD, (h + 1) * HEAD) for h in heads]

    def causal(t):
        r_i = lax.broadcasted_iota(jnp.int32, (2 * t, t), 0)
        c_i = lax.broadcasted_iota(jnp.int32, (2 * t, t), 1)
        return c_i <= jnp.where(r_i >= t, r_i - t, r_i)

    def start(q_rows, t):
        for h in heads:
            qp_ref[h, 0:2 * t, :] = _split_components(q_ref[q_rows, cols[h]])
            m_ref[h, 0:2 * t, :] = jnp.full((2 * t, HEAD), -jnp.inf, F32)
            l_ref[h, 0:2 * t, :] = jnp.zeros((2 * t, HEAD), F32)
            acc_ref[h, 0:2 * t, :] = jnp.zeros((2 * t, HEAD), F32)

    def attend(k_rows, t, mask=None):
        rows = slice(0, 2 * t)
        s = [lax.dot_general(qp_ref[h, rows, :], k_ref[k_rows, cols[h]], NT_DIMS,
                             preferred_element_type=F32) for h in heads]
        p, alpha = [], []
        for h in heads:
            sh = s[h] if mask is None else jnp.where(mask, s[h], -jnp.inf)
            blocks = [sh[:, c:c + HEAD] for c in range(0, sh.shape[1], HEAD)]
            m_old = m_ref[h, rows, :]
            m_new = jnp.maximum(m_old, jnp.max(sh, axis=-1, keepdims=True))
            a = jnp.exp(m_old - m_new)
            ph = [jnp.exp(blk - m_new[:, :blk.shape[1]]) for blk in blocks]
            tot = ph[0]
            for blk in ph[1:]:
                tot = tot + blk
            l_ref[h, rows, :] = a * l_ref[h, rows, :] + jnp.sum(tot, axis=-1, keepdims=True)
            m_ref[h, rows, :] = m_new
            p.append(jnp.concatenate(ph, axis=1).astype(BF16) if len(ph) > 1
                     else ph[0].astype(BF16))
            alpha.append(a)
        for h in heads:
            acc_ref[h, rows, :] = alpha[h] * acc_ref[h, rows, :] + jnp.dot(
                p[h], v_ref[k_rows, cols[h]], preferred_element_type=F32)

    def finish(q_rows, t):
        for h in heads:
            acc = acc_ref[h, 0:2 * t, :]
            l = l_ref[h, 0:2 * t, :]
            o = acc[:t] / l[:t] - lam * (acc[t:] / l[t:])
            ms = jnp.mean(o * o, axis=-1, keepdims=True)
            y = o * lax.rsqrt(ms + EPS) * sw * (1.0 - lam_init) * gate_ref[q_rows, cols[h]]
            y_ref[q_rows, cols[h]] = y.astype(y_ref.dtype)

    meta = pl.ds(0, N_META)

    @pl.when(i == 0)
    def _():
        start(meta, N_META)
        attend(meta, N_META, causal(N_META))
        finish(meta, N_META)

    q_rows = pl.ds(pl.multiple_of(N_META + i * tq, N_META), tq)
    start(q_rows, tq)
    attend(meta, tq)

    def body(j, carry):
        attend(pl.ds(pl.multiple_of(N_META + j * tq, N_META), tq), tq)
        return carry

    lax.fori_loop(0, i, body, 0)
    attend(q_rows, tq, causal(tq))
    finish(q_rows, tq)


def _attn_prompt(aq, kbf, vbf, gatt, sw, lamv, layer, tq):
    bsz, seq, _ = aq.shape
    assert (seq - N_META) % tq == 0
    tok = pl.BlockSpec((None, seq, GROUP), lambda b, i: (b, 0, 0))
    const = lambda b, i: (0, 0)
    return pl.pallas_call(
        functools.partial(_attn_prompt_kernel, tq=tq, lam_init=_lam_init(layer)),
        grid=(bsz, (seq - N_META) // tq),
        in_specs=[tok, tok, tok, tok, pl.BlockSpec((1, HEAD), const),
                  pl.BlockSpec((4, ATT_DH), const)],
        out_specs=tok,
        out_shape=jax.ShapeDtypeStruct((bsz, seq, GROUP), BF16),
        scratch_shapes=[pltpu.VMEM((N_HEADS, 2 * tq, HEAD), BF16),
                        pltpu.VMEM((N_HEADS, 2 * tq, HEAD), F32),
                        pltpu.VMEM((N_HEADS, 2 * tq, HEAD), F32),
                        pltpu.VMEM((N_HEADS, 2 * tq, HEAD), F32)],
        compiler_params=pltpu.CompilerParams(
            dimension_semantics=("parallel", "arbitrary"), vmem_limit_bytes=VMEM_LIMIT),
        name="attn_prompt",
    )(aq, kbf, vbf, gatt, sw, lamv)


def _attn_decode_kernel(pt_ref, q_ref, kn_ref, vn_ref, gate_ref, sw_ref, lamv_ref, *rest,
                        pages, lam_init):
    k_refs = rest[:pages]
    v_refs = rest[pages:2 * pages]
    y_ref = rest[2 * pages]
    q8_ref, m_ref, l_ref, acc_ref = rest[2 * pages + 1:]
    g = pl.program_id(1)
    n_comp = 2 * N_HEADS

    @pl.when(g == 0)
    def _():
        q = jnp.broadcast_to(q_ref[...].astype(F32), (n_comp, GROUP))
        lane = lax.broadcasted_iota(jnp.int32, (n_comp, GROUP), 1)
        lo = lax.broadcasted_iota(jnp.int32, (n_comp, GROUP), 0) * ATT_DH
        q8 = jnp.where((lane >= lo) & (lane < lo + ATT_DH), q, 0.0).astype(BF16)
        q8_ref[...] = q8
        kn = jnp.broadcast_to(kn_ref[...].astype(BF16), (n_comp, GROUP))
        s_new = lax.dot_general(kn, q8, NT_DIMS, preferred_element_type=F32)
        m_ref[...] = s_new[0:1]
        l_ref[...] = jnp.ones_like(l_ref)
        row0 = lax.broadcasted_iota(jnp.int32, (8, HEAD), 0) == 0
        for j in range(n_comp):
            h = j // 2
            vn = jnp.broadcast_to(vn_ref[:, h * HEAD:(h + 1) * HEAD], (8, HEAD))
            acc_ref[j] = jnp.where(row0, vn, 0.0)

    def head_rows(ref, h):
        return ref[pl.ds(h, PAGE, stride=N_HEADS), :]

    q8 = q8_ref[...]
    scores = []
    for p in range(pages):
        kb = jnp.concatenate([head_rows(k_refs[p], h) for h in range(N_HEADS)], axis=1)
        scores.append(lax.dot_general(kb.astype(BF16), q8, NT_DIMS,
                                      preferred_element_type=F32))
    m_old = m_ref[...]
    m_new = m_old
    for s in scores:
        m_new = jnp.maximum(m_new, jnp.max(s, axis=0, keepdims=True))
    alpha = jnp.exp(m_old - m_new)
    probs = [jnp.exp(s - m_new) for s in scores]
    l_new = alpha * l_ref[...]
    for p in probs:
        l_new = l_new + jnp.sum(p, axis=0, keepdims=True)
    m_ref[...] = m_new
    l_ref[...] = l_new
    p_all = jnp.concatenate(probs, axis=0).astype(BF16)
    sel_row = lax.broadcasted_iota(jnp.int32, (n_comp, 2 * HEAD), 0)
    sel_col = lax.broadcasted_iota(jnp.int32, (n_comp, 2 * HEAD), 1)
    for h in range(N_HEADS):
        sel = jnp.where(sel_row == jnp.where(sel_col < HEAD, 2 * h, 2 * h + 1), 1.0, 0.0)
        pb = jnp.dot(p_all, sel.astype(BF16), preferred_element_type=F32)
        part = [acc_ref[2 * h + c] * alpha[:, 2 * h + c:2 * h + c + 1] for c in range(2)]
        for p in range(pages):
            vh = head_rows(v_refs[p], h)
            for c in range(2):
                pv = pb[p * PAGE:(p + 1) * PAGE, c * HEAD:(c + 1) * HEAD] * vh
                part[c] = part[c] + jnp.sum(pv.reshape(PAGE // 8, 8, HEAD), axis=0)
        for c in range(2):
            acc_ref[2 * h + c] = part[c]

    @pl.when(g == pl.num_programs(1) - 1)
    def _():
        lam = _lam(lamv_ref[...], lam_init)
        l = l_ref[...]
        sw = sw_ref[...]
        for h in range(N_HEADS):
            o1 = jnp.sum(acc_ref[2 * h], axis=0, keepdims=True) / l[:, 2 * h:2 * h + 1]
            o2 = jnp.sum(acc_ref[2 * h + 1], axis=0, keepdims=True) / l[:, 2 * h + 1:2 * h + 2]
            o = o1 - lam * o2
            ms = jnp.mean(o * o, axis=-1, keepdims=True)
            cols = slice(h * HEAD, (h + 1) * HEAD)
            y = o * lax.rsqrt(ms + EPS) * sw * (1.0 - lam_init) * gate_ref[:, cols]
            y_ref[:, cols] = y.astype(y_ref.dtype)


def _attn_decode(page_table, aq, kn, vn, gatt, sw, lamv, cache_k, cache_v, layer, pages):
    n_seq, n_pages = page_table.shape
    assert n_pages % pages == 0
    tok = pl.BlockSpec((None, 1, GROUP), lambda b, g, pt: (b, 0, 0))
    const = lambda b, g, pt: (0, 0)

    def page_spec(p):
        return pl.BlockSpec((None, None, PAGE * N_HEADS, HEAD),
                            lambda b, g, pt: (layer, pt[b, g * pages + p], 0, 0))

    n_comp = 2 * N_HEADS
    grid_spec = pltpu.PrefetchScalarGridSpec(
        num_scalar_prefetch=1,
        grid=(n_seq, n_pages // pages),
        in_specs=[tok, tok, tok, tok, pl.BlockSpec((1, HEAD), const),
                  pl.BlockSpec((4, ATT_DH), const)]
                 + [page_spec(p) for p in range(pages)] * 2,
        out_specs=tok,
        scratch_shapes=[pltpu.VMEM((n_comp, GROUP), BF16), pltpu.VMEM((1, n_comp), F32),
                        pltpu.VMEM((1, n_comp), F32), pltpu.VMEM((n_comp, 8, HEAD), F32)],
    )
    r3 = lambda a: a.reshape(n_seq, 1, GROUP)
    return pl.pallas_call(
        functools.partial(_attn_decode_kernel, pages=pages, lam_init=_lam_init(layer)),
        grid_spec=grid_spec,
        out_shape=jax.ShapeDtypeStruct((n_seq, 1, GROUP), BF16),
        compiler_params=pltpu.CompilerParams(
            dimension_semantics=("parallel", "arbitrary"), vmem_limit_bytes=VMEM_LIMIT),
        name="attn_decode",
    )(page_table, r3(aq), r3(kn), r3(vn), r3(gatt), sw, lamv,
      *([cache_k] * pages), *([cache_v] * pages)).reshape(n_seq, GROUP)


def _outproj_kernel(h_ref, yr_ref, ya_ref, w_ref, fw_ref, o_ref, *, final):
    h = (h_ref[...]
         + jnp.dot(yr_ref[...], w_ref[:GROUP, :], preferred_element_type=F32)
         + jnp.dot(ya_ref[...], w_ref[GROUP:, :], preferred_element_type=F32))
    if final:
        ms = jnp.mean(h * h, axis=-1, keepdims=True)
        h = h * lax.rsqrt(ms + EPS) * fw_ref[...]
    o_ref[...] = h


def _outproj(h, y_rec, y_att, w, fw, final, tm):
    m = h.shape[0]
    row = lambda i: (i, 0)
    const = lambda i: (0, 0)
    return pl.pallas_call(
        functools.partial(_outproj_kernel, final=final),
        grid=(m // tm,),
        in_specs=[pl.BlockSpec((tm, D_MODEL), row), pl.BlockSpec((tm, GROUP), row),
                  pl.BlockSpec((tm, GROUP), row), pl.BlockSpec((2 * GROUP, D_MODEL), const),
                  pl.BlockSpec((1, D_MODEL), const)],
        out_specs=pl.BlockSpec((tm, D_MODEL), row),
        out_shape=jax.ShapeDtypeStruct((m, D_MODEL), F32),
        compiler_params=pltpu.CompilerParams(
            dimension_semantics=("parallel",), vmem_limit_bytes=VMEM_LIMIT),
        name=f"outproj_m{m}",
    )(h, y_rec, y_att, w, fw)


PROMPT_TM = 384
PROMPT_TQ = 256
HGRN_HEADS = 4
DECODE_PAGES = 8


def kernel(x_prompt, x_sample, cache_k, cache_v, state_hgrn, page_table, meta_tokens, norm_w, w_in,
           w_out, lb_logits, hgrn_norm_w, subln_w, lam_q1, lam_k1, lam_q2, lam_k2, final_norm_w):
    depth = w_in.shape[0]
    bp, seq_x, _ = x_prompt.shape
    seq = seq_x + N_META
    n_dec = x_sample.shape[0]
    n_phys = cache_k.shape[1]

    p = jax.nn.softmax(lb_logits.astype(F32), axis=0)
    c = jnp.cumsum(p, axis=0)
    lb = c - c[0:1]
    loglb, log1mlb, omlb = jnp.log(lb), jnp.log1p(-lb), 1.0 - lb
    w_in_bf = w_in.astype(BF16)
    w_out_bf = w_out.astype(BF16)
    lamv = jnp.stack([lam_q1, lam_k1, lam_q2, lam_k2], axis=1).astype(F32)
    fw = final_norm_w.reshape(1, D_MODEL)
    ck = cache_k.reshape(depth, n_phys, PAGE * N_HEADS, HEAD)
    cv = cache_v.reshape(depth, n_phys, PAGE * N_HEADS, HEAD)

    meta = jnp.broadcast_to(meta_tokens[None].astype(x_prompt.dtype), (bp, N_META, D_MODEL))
    hp = jnp.concatenate([meta, x_prompt], axis=1).reshape(bp * seq, D_MODEL)
    hs = x_sample.reshape(n_dec, D_MODEL)
    zero_state = jnp.zeros((bp, N_HEADS, HEAD, HEAD), F32)

    kv_p = None
    kv_s = None
    sp, ss = [], []
    for l in range(depth):
        row = lambda a: a[l].reshape(1, -1)
        par = (row(norm_w), w_in_bf[l], row(loglb), row(log1mlb), row(omlb))
        final = l == depth - 1

        rq, logf, kin, ri, grec, aq, k_all, v_all, kbf, vbf, gatt = _inproj(
            hp, *par, l, depth, kv_p, PROMPT_TM)
        kv_p = (k_all, v_all)
        b3 = lambda a: a.reshape(bp, seq, GROUP)
        y_rec, s_p = _hgrn(b3(rq), b3(logf), b3(kin), b3(ri), b3(grec), zero_state,
                           row(hgrn_norm_w), HGRN_HEADS, "hgrn_prompt")
        y_att = _attn_prompt(b3(aq), b3(kbf), b3(vbf), b3(gatt), row(subln_w), lamv[l], l,
                             PROMPT_TQ)
        hp = _outproj(hp, y_rec.reshape(bp * seq, GROUP), y_att.reshape(bp * seq, GROUP),
                      w_out_bf[l], fw, final, PROMPT_TM)
        sp.append(s_p)

        rq, logf, kin, ri, grec, aq, k_all, v_all, kbf, vbf, gatt = _inproj(
            hs, *par, l, depth, kv_s, n_dec)
        kv_s = (k_all, v_all)
        pad = lambda a: jnp.pad(a.reshape(n_dec, 1, GROUP), ((0, 0), (0, SUB - 1), (0, 0)))
        y_rec, s_s = _hgrn(pad(rq), pad(logf), pad(kin), pad(ri), pad(grec), state_hgrn[l],
                           row(hgrn_norm_w), HGRN_HEADS, "hgrn_decode")
        y_att = _attn_decode(page_table, aq, k_all[l].reshape(n_dec, GROUP),
                             v_all[l].reshape(n_dec, GROUP), gatt, row(subln_w), lamv[l],
                             ck, cv, l, DECODE_PAGES)
        hs = _outproj(hs, y_rec[:, 0, :], y_att, w_out_bf[l], fw, final, n_dec)
        ss.append(s_s)

    n_ah = GROUP // HEAD
    y_prompt = hp.reshape(bp, seq, D_MODEL)[:, N_META:]
    y_sample = hs.reshape(n_dec, 1, D_MODEL)
    return (y_prompt, y_sample,
            kv_p[0].reshape(depth, bp, seq, n_ah, HEAD), kv_p[1].reshape(depth, bp, seq, n_ah, HEAD),
            jnp.stack(sp),
            kv_s[0].reshape(depth, n_dec, 1, n_ah, HEAD), kv_s[1].reshape(depth, n_dec, 1, n_ah, HEAD),
            jnp.stack(ss))
```

```python
import functools
import math

import jax
import jax.numpy as jnp
from jax import lax
from jax.experimental import pallas as pl
from jax.experimental.pallas import tpu as pltpu

F32 = jnp.float32
BF16 = jnp.bfloat16

D_MODEL = 1024
N_META = 16
GROUP = 512
HEAD = 128
N_HEADS = 4
ATT_DH = 64
PAGE = 128
CHUNK = 64
SUB = 16
EPS = 1e-6
LOG2E = 1.4426950408889634
VMEM_LIMIT = 52 * 1024 * 1024

NT_DIMS = (((1,), (1,)), ((), ()))
TN_DIMS = (((0,), (0,)), ((), ()))


def _lam_init(layer):
    return 0.8 - 0.6 * math.exp(-0.3 * layer)


def _sigmoid(x):
    return 1.0 / (1.0 + jnp.exp(-x))


def _inproj_kernel(*refs, aliased):
    if aliased:
        refs = refs[:6] + refs[8:]
    (x_ref, nw_ref, w_ref, loglb_ref, log1mlb_ref, omlb_ref,
     rq_ref, logf_ref, kin_ref, ri_ref, grec_ref, aq_ref, k_ref, v_ref, kbf_ref, vbf_ref,
     gatt_ref) = refs
    x = x_ref[...]
    ms = jnp.mean(x * x, axis=-1, keepdims=True)
    xn = (x * lax.rsqrt(ms + EPS) * nw_ref[...]).astype(BF16)

    def proj(g):
        return jnp.dot(xn, w_ref[:, g * GROUP:(g + 1) * GROUP], preferred_element_type=F32)

    rq_ref[...] = proj(0).astype(BF16)
    z = proj(1)
    e = jnp.exp(-jnp.abs(z))
    log_sig = jnp.minimum(z, 0.0) - jnp.log(1.0 + e)
    a = loglb_ref[...]
    b = log1mlb_ref[...] + log_sig
    logf_ref[...] = jnp.maximum(a, b) + jnp.log(1.0 + jnp.exp(-jnp.abs(a - b)))
    kin_ref[...] = omlb_ref[...] * (jnp.where(z >= 0.0, e, 1.0) / (1.0 + e))
    ri_ref[...] = proj(2).astype(BF16)
    g = proj(3)
    grec_ref[...] = g * _sigmoid(g)
    aq_ref[...] = (proj(4) * (ATT_DH ** -0.5)).astype(BF16)
    tm = x.shape[0]
    k = proj(5)
    v = proj(6)
    for h in range(N_HEADS):
        k_ref[pl.ds(h, tm, stride=N_HEADS), :] = k[:, h * HEAD:(h + 1) * HEAD]
        v_ref[pl.ds(h, tm, stride=N_HEADS), :] = v[:, h * HEAD:(h + 1) * HEAD]
    kbf_ref[...] = k.astype(BF16)
    vbf_ref[...] = v.astype(BF16)
    g = proj(7)
    gatt_ref[...] = g * _sigmoid(g)


def _inproj(h, nw, w, loglb, log1mlb, omlb, layer, depth, kv_bufs, tm):
    m = h.shape[0]
    grid = (m // tm,)
    row = lambda i: (i, 0)
    const = lambda i: (0, 0)
    in_specs = [
        pl.BlockSpec((tm, D_MODEL), row),
        pl.BlockSpec((1, D_MODEL), const),
        pl.BlockSpec((D_MODEL, 8 * GROUP), const),
        pl.BlockSpec((1, GROUP), const),
        pl.BlockSpec((1, GROUP), const),
        pl.BlockSpec((1, GROUP), const),
    ]
    args = [h, nw, w, loglb, log1mlb, omlb]
    aliases = {}
    if kv_bufs is not None:
        in_specs += [pl.BlockSpec(memory_space=pl.ANY)] * 2
        args += list(kv_bufs)
        aliases = {6: 6, 7: 7}
    tile = lambda dt: jax.ShapeDtypeStruct((m, GROUP), dt)
    stacked = jax.ShapeDtypeStruct((depth, m * N_HEADS, HEAD), F32)
    out_shape = [tile(BF16), tile(F32), tile(F32), tile(BF16), tile(F32), tile(BF16),
                 stacked, stacked, tile(BF16), tile(BF16), tile(F32)]
    tile_spec = pl.BlockSpec((tm, GROUP), row)
    stacked_spec = pl.BlockSpec((None, tm * N_HEADS, HEAD), lambda i: (layer, i, 0))
    out_specs = [tile_spec] * 6 + [stacked_spec] * 2 + [tile_spec] * 3
    return pl.pallas_call(
        functools.partial(_inproj_kernel, aliased=kv_bufs is not None),
        grid=grid, in_specs=in_specs, out_specs=out_specs, out_shape=out_shape,
        input_output_aliases=aliases,
        compiler_params=pltpu.CompilerParams(
            dimension_semantics=("parallel",), vmem_limit_bytes=VMEM_LIMIT),
        name=f"inproj_m{m}",
    )(*args)


def _hgrn_kernel(q_ref, g_ref, k_ref, v_ref, gate_ref, s0_ref, nw_ref, y_ref, sout_ref,
                 st_ref, cj_ref, vf_ref, *, seq, heads):
    first = seq % CHUNK
    n_full = seq // CHUNK
    nw = nw_ref[...]
    row8 = lax.broadcasted_iota(jnp.int32, (8, 1), 0)

    def chunk(start, c):
        sl = pl.ds(start, c)
        r_i = lax.broadcasted_iota(jnp.int32, (c, c), 0)
        c_i = lax.broadcasted_iota(jnp.int32, (c, c), 1)
        tril = (r_i >= c_i).astype(F32)
        hs = range(heads)
        cols = [slice(hh * HEAD, (hh + 1) * HEAD) for hh in hs]
        b2 = [jnp.dot(tril, g_ref[sl, cols[hh]], precision=lax.Precision.HIGHEST,
                      preferred_element_type=F32) * LOG2E for hh in hs]
        q = [q_ref[sl, cols[hh]].astype(F32) for hh in hs]
        k = [k_ref[sl, cols[hh]].astype(F32) for hh in hs]
        v = [v_ref[sl, cols[hh]] for hh in hs]
        o = [lax.dot_general((q[hh] * jnp.exp2(b2[hh])).astype(BF16), st_ref[hh].astype(BF16),
                             NT_DIMS, preferred_element_type=F32) for hh in hs]
        for hh in hs:
            cj_ref[hh, 0:c, :] = b2[hh] - jnp.log2(k[hh])
            vf_ref[hh, 0:c, :] = v[hh].astype(F32)
        off = [[None] * (c // SUB) for _ in hs]
        for s in range(1, c // SUB):
            lo = s * SUB
            for hh in hs:
                edge = b2[hh][lo - 1:lo]
                qh = (q[hh][lo:lo + SUB] * jnp.exp2(b2[hh][lo:lo + SUB] - edge)).astype(BF16)
                kh = (k[hh][:lo] * jnp.exp2(edge - b2[hh][:lo])).astype(BF16)
                a = lax.dot_general(qh, kh, NT_DIMS, preferred_element_type=F32)
                off[hh][s] = jnp.dot(a.astype(BF16), v[hh][:lo], preferred_element_type=F32)
        for hh in hs:
            parts = []
            for r in range(c // 8):
                b_r = b2[hh][8 * r:8 * r + 8]
                q_r = q[hh][8 * r:8 * r + 8]
                o_r = o[hh][8 * r:8 * r + 8]
                lo = (8 * r // SUB) * SUB
                if lo > 0:
                    o_r = o_r + off[hh][lo // SUB][8 * r - lo:8 * r - lo + 8]
                for jj in range(lo, 8 * r + 8):
                    col = jnp.sum(q_r * jnp.exp2(b_r - cj_ref[hh, jj:jj + 1, :]),
                                  axis=-1, keepdims=True)
                    if jj >= 8 * r:
                        col = jnp.where(row8 >= jj - 8 * r, col, 0.0)
                    o_r = o_r + col * vf_ref[hh, jj:jj + 1, :]
                parts.append(o_r)
            oo = jnp.concatenate(parts, axis=0)
            ms = jnp.mean(oo * oo, axis=-1, keepdims=True)
            y = oo * lax.rsqrt(ms + EPS) * nw * gate_ref[sl, cols[hh]].astype(F32)
            y_ref[sl, cols[hh]] = y.astype(y_ref.dtype)
        for hh in hs:
            b_last = b2[hh][c - 1:c]
            kd = (k[hh] * jnp.exp2(b_last - b2[hh])).astype(BF16)
            upd = lax.dot_general(v[hh], kd, TN_DIMS, preferred_element_type=F32)
            st_ref[hh] = st_ref[hh] * jnp.exp2(b_last) + upd

    for hh in range(heads):
        st_ref[hh] = s0_ref[hh].T
    if first:
        chunk(0, first)
    if n_full:
        def body(j, carry):
            chunk(pl.multiple_of(first + j * CHUNK, SUB), CHUNK)
            return carry
        lax.fori_loop(0, n_full, body, 0)
    for hh in range(heads):
        sout_ref[hh] = st_ref[hh].T


def _hgrn(rq, logf, kin, ri, grec, s0, nw, heads, name):
    bsz, seq, _ = rq.shape
    assert seq % SUB == 0 and N_HEADS % heads == 0
    tok = pl.BlockSpec((None, seq, heads * HEAD), lambda b, h: (b, 0, h))
    state = pl.BlockSpec((None, heads, HEAD, HEAD), lambda b, h: (b, h, 0, 0))
    return pl.pallas_call(
        functools.partial(_hgrn_kernel, seq=seq, heads=heads),
        grid=(bsz, N_HEADS // heads),
        in_specs=[tok, tok, tok, tok, tok, state, pl.BlockSpec((1, HEAD), lambda b, h: (0, 0))],
        out_specs=[tok, state],
        out_shape=[jax.ShapeDtypeStruct((bsz, seq, GROUP), BF16),
                   jax.ShapeDtypeStruct(s0.shape, F32)],
        scratch_shapes=[pltpu.VMEM((heads, HEAD, HEAD), F32), pltpu.VMEM((heads, CHUNK, HEAD), F32),
                        pltpu.VMEM((heads, CHUNK, HEAD), F32)],
        compiler_params=pltpu.CompilerParams(
            dimension_semantics=("parallel", "parallel"), vmem_limit_bytes=VMEM_LIMIT),
        name=name,
    )(rq, logf, kin, ri, grec, s0, nw)


def _lam(lamv, lam_init):
    return (jnp.exp(jnp.sum(lamv[0:1] * lamv[1:2], axis=-1, keepdims=True))
            - jnp.exp(jnp.sum(lamv[2:3] * lamv[3:4], axis=-1, keepdims=True)) + lam_init)


def _split_components(q):
    lane = lax.broadcasted_iota(jnp.int32, q.shape, 1)
    qf = q.astype(F32)
    return jnp.concatenate([jnp.where(lane < ATT_DH, qf, 0.0), jnp.where(lane >= ATT_DH, qf, 0.0)],
                           axis=0).astype(BF16)


def _attn_prompt_kernel(q_ref, k_ref, v_ref, gate_ref, sw_ref, lamv_ref, y_ref,
                        qp_ref, m_ref, l_ref, acc_ref, *, tq, lam_init):
    i = pl.program_id(1)
    lam = _lam(lamv_ref[...], lam_init)
    sw = sw_ref[...]
    heads = range(N_HEADS)
    cols = [slice(h * HEAD, (h + 1) * HEAD) for h in heads]

    def causal(t):
        r_i = lax.broadcasted_iota(jnp.int32, (2 * t, t), 0)
        c_i = lax.broadcasted_iota(jnp.int32, (2 * t, t), 1)
        return c_i <= jnp.where(r_i >= t, r_i - t, r_i)

    def start(q_rows, t):
        for h in heads:
            qp_ref[h, 0:2 * t, :] = _split_components(q_ref[q_rows, cols[h]])
            m_ref[h, 0:2 * t, :] = jnp.full((2 * t, HEAD), -jnp.inf, F32)
            l_ref[h, 0:2 * t, :] = jnp.zeros((2 * t, HEAD), F32)
            acc_ref[h, 0:2 * t, :] = jnp.zeros((2 * t, HEAD), F32)

    def attend(k_rows, t, mask=None):
        rows = slice(0, 2 * t)
        s = [lax.dot_general(qp_ref[h, rows, :], k_ref[k_rows, cols[h]], NT_DIMS,
                             preferred_element_type=F32) for h in heads]
        p, alpha = [], []
        for h in heads:
            sh = s[h] if mask is None else jnp.where(mask, s[h], -jnp.inf)
            blocks = [sh[:, c:c + HEAD] for c in range(0, sh.shape[1], HEAD)]
            m_old = m_ref[h, rows, :]
            m_new = jnp.maximum(m_old, jnp.max(sh, axis=-1, keepdims=True))
            a = jnp.exp(m_old - m_new)
            ph = [jnp.exp(blk - m_new[:, :blk.shape[1]]) for blk in blocks]
            tot = ph[0]
            for blk in ph[1:]:
                tot = tot + blk
            l_ref[h, rows, :] = a * l_ref[h, rows, :] + jnp.sum(tot, axis=-1, keepdims=True)
            m_ref[h, rows, :] = m_new
            p.append(jnp.concatenate(ph, axis=1).astype(BF16) if len(ph) > 1
                     else ph[0].astype(BF16))
            alpha.append(a)
        for h in heads:
            acc_ref[h, rows, :] = alpha[h] * acc_ref[h, rows, :] + jnp.dot(
                p[h], v_ref[k_rows, cols[h]], preferred_element_type=F32)

    def finish(q_rows, t):
        for h in heads:
            acc = acc_ref[h, 0:2 * t, :]
            l = l_ref[h, 0:2 * t, :]
            o = acc[:t] / l[:t] - lam * (acc[t:] / l[t:])
            ms = jnp.mean(o * o, axis=-1, keepdims=True)
            y = o * lax.rsqrt(ms + EPS) * sw * (1.0 - lam_init) * gate_ref[q_rows, cols[h]]
            y_ref[q_rows, cols[h]] = y.astype(y_ref.dtype)

    meta = pl.ds(0, N_META)

    @pl.when(i == 0)
    def _():
        start(meta, N_META)
        attend(meta, N_META, causal(N_META))
        finish(meta, N_META)

    q_rows = pl.ds(pl.multiple_of(N_META + i * tq, N_META), tq)
    start(q_rows, tq)
    attend(meta, tq)

    def body(j, carry):
        attend(pl.ds(pl.multiple_of(N_META + j * tq, N_META), tq), tq)
        return carry

    lax.fori_loop(0, i, body, 0)
    attend(q_rows, tq, causal(tq))
    finish(q_rows, tq)


def _attn_prompt(aq, kbf, vbf, gatt, sw, lamv, layer, tq):
    bsz, seq, _ = aq.shape
    assert (seq - N_META) % tq == 0
    tok = pl.BlockSpec((None, seq, GROUP), lambda b, i: (b, 0, 0))
    const = lambda b, i: (0, 0)
    return pl.pallas_call(
        functools.partial(_attn_prompt_kernel, tq=tq, lam_init=_lam_init(layer)),
        grid=(bsz, (seq - N_META) // tq),
        in_specs=[tok, tok, tok, tok, pl.BlockSpec((1, HEAD), const),
                  pl.BlockSpec((4, ATT_DH), const)],
        out_specs=tok,
        out_shape=jax.ShapeDtypeStruct((bsz, seq, GROUP), BF16),
        scratch_shapes=[pltpu.VMEM((N_HEADS, 2 * tq, HEAD), BF16),
                        pltpu.VMEM((N_HEADS, 2 * tq, HEAD), F32),
                        pltpu.VMEM((N_HEADS, 2 * tq, HEAD), F32),
                        pltpu.VMEM((N_HEADS, 2 * tq, HEAD), F32)],
        compiler_params=pltpu.CompilerParams(
            dimension_semantics=("parallel", "arbitrary"), vmem_limit_bytes=VMEM_LIMIT),
        name="attn_prompt",
    )(aq, kbf, vbf, gatt, sw, lamv)


def _attn_decode_kernel(pt_ref, q_ref, kn_ref, vn_ref, gate_ref, sw_ref, lamv_ref, *rest,
                        pages, lam_init):
    k_refs = rest[:pages]
    v_refs = rest[pages:2 * pages]
    y_ref = rest[2 * pages]
    q8_ref, m_ref, l_ref, acc_ref = rest[2 * pages + 1:]
    g = pl.program_id(1)
    n_comp = 2 * N_HEADS

    @pl.when(g == 0)
    def _():
        q = jnp.broadcast_to(q_ref[...].astype(F32), (n_comp, GROUP))
        lane = lax.broadcasted_iota(jnp.int32, (n_comp, GROUP), 1)
        lo = lax.broadcasted_iota(jnp.int32, (n_comp, GROUP), 0) * ATT_DH
        q8 = jnp.where((lane >= lo) & (lane < lo + ATT_DH), q, 0.0).astype(BF16)
        q8_ref[...] = q8
        kn = jnp.broadcast_to(kn_ref[...].astype(BF16), (n_comp, GROUP))
        s_new = lax.dot_general(kn, q8, NT_DIMS, preferred_element_type=F32)
        m_ref[...] = s_new[0:1]
        l_ref[...] = jnp.ones_like(l_ref)
        row0 = lax.broadcasted_iota(jnp.int32, (8, HEAD), 0) == 0
        for j in range(n_comp):
            h = j // 2
            vn = jnp.broadcast_to(vn_ref[:, h * HEAD:(h + 1) * HEAD], (8, HEAD))
            acc_ref[j] = jnp.where(row0, vn, 0.0)

    def head_rows(ref, h):
        return ref[pl.ds(h, PAGE, stride=N_HEADS), :]

    q8 = q8_ref[...]
    scores = []
    for p in range(pages):
        kb = jnp.concatenate([head_rows(k_refs[p], h) for h in range(N_HEADS)], axis=1)
        scores.append(lax.dot_general(kb.astype(BF16), q8, NT_DIMS,
                                      preferred_element_type=F32))
    m_old = m_ref[...]
    m_new = m_old
    for s in scores:
        m_new = jnp.maximum(m_new, jnp.max(s, axis=0, keepdims=True))
    alpha = jnp.exp(m_old - m_new)
    probs = [jnp.exp(s - m_new) for s in scores]
    l_new = alpha * l_ref[...]
    for p in probs:
        l_new = l_new + jnp.sum(p, axis=0, keepdims=True)
    m_ref[...] = m_new
    l_ref[...] = l_new
    p_all = jnp.concatenate(probs, axis=0).astype(BF16)
    sel_row = lax.broadcasted_iota(jnp.int32, (n_comp, 2 * HEAD), 0)
    sel_col = lax.broadcasted_iota(jnp.int32, (n_comp, 2 * HEAD), 1)
    for h in range(N_HEADS):
        sel = jnp.where(sel_row == jnp.where(sel_col < HEAD, 2 * h, 2 * h + 1), 1.0, 0.0)
        pb = jnp.dot(p_all, sel.astype(BF16), preferred_element_type=F32)
        part = [acc_ref[2 * h + c] * alpha[:, 2 * h + c:2 * h + c + 1] for c in range(2)]
        for p in range(pages):
            vh = head_rows(v_refs[p], h)
            for c in range(2):
                pv = pb[p * PAGE:(p + 1) * PAGE, c * HEAD:(c + 1) * HEAD] * vh
                part[c] = part[c] + jnp.sum(pv.reshape(PAGE // 8, 8, HEAD), axis=0)
        for c in range(2):
            acc_ref[2 * h + c] = part[c]

    @pl.when(g == pl.num_programs(1) - 1)
    def _():
        lam = _lam(lamv_ref[...], lam_init)
        l = l_ref[...]
        sw = sw_ref[...]
        for h in range(N_HEADS):
            o1 = jnp.sum(acc_ref[2 * h], axis=0, keepdims=True) / l[:, 2 * h:2 * h + 1]
            o2 = jnp.sum(acc_ref[2 * h + 1], axis=0, keepdims=True) / l[:, 2 * h + 1:2 * h + 2]
            o = o1 - lam * o2
            ms = jnp.mean(o * o, axis=-1, keepdims=True)
            cols = slice(h * HEAD, (h + 1) * HEAD)
            y = o * lax.rsqrt(ms + EPS) * sw * (1.0 - lam_init) * gate_ref[:, cols]
            y_ref[:, cols] = y.astype(y_ref.dtype)


def _attn_decode(page_table, aq, kn, vn, gatt, sw, lamv, cache_k, cache_v, layer, pages):
    n_seq, n_pages = page_table.shape
    assert n_pages % pages == 0
    tok = pl.BlockSpec((None, 1, GROUP), lambda b, g, pt: (b, 0, 0))
    const = lambda b, g, pt: (0, 0)

    def page_spec(p):
        return pl.BlockSpec((None, None, PAGE * N_HEADS, HEAD),
                            lambda b, g, pt: (layer, pt[b, g * pages + p], 0, 0))

    n_comp = 2 * N_HEADS
    grid_spec = pltpu.PrefetchScalarGridSpec(
        num_scalar_prefetch=1,
        grid=(n_seq, n_pages // pages),
        in_specs=[tok, tok, tok, tok, pl.BlockSpec((1, HEAD), const),
                  pl.BlockSpec((4, ATT_DH), const)]
                 + [page_spec(p) for p in range(pages)] * 2,
        out_specs=tok,
        scratch_shapes=[pltpu.VMEM((n_comp, GROUP), BF16), pltpu.VMEM((1, n_comp), F32),
                        pltpu.VMEM((1, n_comp), F32), pltpu.VMEM((n_comp, 8, HEAD), F32)],
    )
    r3 = lambda a: a.reshape(n_seq, 1, GROUP)
    return pl.pallas_call(
        functools.partial(_attn_decode_kernel, pages=pages, lam_init=_lam_init(layer)),
        grid_spec=grid_spec,
        out_shape=jax.ShapeDtypeStruct((n_seq, 1, GROUP), BF16),
        compiler_params=pltpu.CompilerParams(
            dimension_semantics=("parallel", "arbitrary"), vmem_limit_bytes=VMEM_LIMIT),
        name="attn_decode",
    )(page_table, r3(aq), r3(kn), r3(vn), r3(gatt), sw, lamv,
      *([cache_k] * pages), *([cache_v] * pages)).reshape(n_seq, GROUP)


def _outproj_kernel(h_ref, yr_ref, ya_ref, w_ref, fw_ref, o_ref, *, final):
    h = (h_ref[...]
         + jnp.dot(yr_ref[...], w_ref[:GROUP, :], preferred_element_type=F32)
         + jnp.dot(ya_ref[...], w_ref[GROUP:, :], preferred_element_type=F32))
    if final:
        ms = jnp.mean(h * h, axis=-1, keepdims=True)
        h = h * lax.rsqrt(ms + EPS) * fw_ref[...]
    o_ref[...] = h


def _outproj(h, y_rec, y_att, w, fw, final, tm):
    m = h.shape[0]
    row = lambda i: (i, 0)
    const = lambda i: (0, 0)
    return pl.pallas_call(
        functools.partial(_outproj_kernel, final=final),
        grid=(m // tm,),
        in_specs=[pl.BlockSpec((tm, D_MODEL), row), pl.BlockSpec((tm, GROUP), row),
                  pl.BlockSpec((tm, GROUP), row), pl.BlockSpec((2 * GROUP, D_MODEL), const),
                  pl.BlockSpec((1, D_MODEL), const)],
        out_specs=pl.BlockSpec((tm, D_MODEL), row),
        out_shape=jax.ShapeDtypeStruct((m, D_MODEL), F32),
        compiler_params=pltpu.CompilerParams(
            dimension_semantics=("parallel",), vmem_limit_bytes=VMEM_LIMIT),
        name=f"outproj_m{m}",
    )(h, y_rec, y_att, w, fw)


PROMPT_TM = 384
PROMPT_TQ = 256
HGRN_HEADS = 4
DECODE_PAGES = 16


def kernel(x_prompt, x_sample, cache_k, cache_v, state_hgrn, page_table, meta_tokens, norm_w, w_in,
           w_out, lb_logits, hgrn_norm_w, subln_w, lam_q1, lam_k1, lam_q2, lam_k2, final_norm_w):
    depth = w_in.shape[0]
    bp, seq_x, _ = x_prompt.shape
    seq = seq_x + N_META
    n_dec = x_sample.shape[0]
    n_phys = cache_k.shape[1]

    p = jax.nn.softmax(lb_logits.astype(F32), axis=0)
    c = jnp.cumsum(p, axis=0)
    lb = c - c[0:1]
    loglb, log1mlb, omlb = jnp.log(lb), jnp.log1p(-lb), 1.0 - lb
    w_in_bf = w_in.astype(BF16)
    w_out_bf = w_out.astype(BF16)
    lamv = jnp.stack([lam_q1, lam_k1, lam_q2, lam_k2], axis=1).astype(F32)
    fw = final_norm_w.reshape(1, D_MODEL)
    ck = cache_k.reshape(depth, n_phys, PAGE * N_HEADS, HEAD)
    cv = cache_v.reshape(depth, n_phys, PAGE * N_HEADS, HEAD)

    meta = jnp.broadcast_to(meta_tokens[None].astype(x_prompt.dtype), (bp, N_META, D_MODEL))
    hp = jnp.concatenate([meta, x_prompt], axis=1).reshape(bp * seq, D_MODEL)
    hs = x_sample.reshape(n_dec, D_MODEL)
    zero_state = jnp.zeros((bp, N_HEADS, HEAD, HEAD), F32)

    kv_p = None
    kv_s = None
    sp, ss = [], []
    for l in range(depth):
        row = lambda a: a[l].reshape(1, -1)
        par = (row(norm_w), w_in_bf[l], row(loglb), row(log1mlb), row(omlb))
        final = l == depth - 1

        rq, logf, kin, ri, grec, aq, k_all, v_all, kbf, vbf, gatt = _inproj(
            hp, *par, l, depth, kv_p, PROMPT_TM)
        kv_p = (k_all, v_all)
        b3 = lambda a: a.reshape(bp, seq, GROUP)
        y_rec, s_p = _hgrn(b3(rq), b3(logf), b3(kin), b3(ri), b3(grec), zero_state,
                           row(hgrn_norm_w), HGRN_HEADS, "hgrn_prompt")
        y_att = _attn_prompt(b3(aq), b3(kbf), b3(vbf), b3(gatt), row(subln_w), lamv[l], l,
                             PROMPT_TQ)
        hp = _outproj(hp, y_rec.reshape(bp * seq, GROUP), y_att.reshape(bp * seq, GROUP),
                      w_out_bf[l], fw, final, PROMPT_TM)
        sp.append(s_p)

        rq, logf, kin, ri, grec, aq, k_all, v_all, kbf, vbf, gatt = _inproj(
            hs, *par, l, depth, kv_s, n_dec)
        kv_s = (k_all, v_all)
        pad = lambda a: jnp.pad(a.reshape(n_dec, 1, GROUP), ((0, 0), (0, SUB - 1), (0, 0)))
        y_rec, s_s = _hgrn(pad(rq), pad(logf), pad(kin), pad(ri), pad(grec), state_hgrn[l],
                           row(hgrn_norm_w), HGRN_HEADS, "hgrn_decode")
        y_att = _attn_decode(page_table, aq, k_all[l].reshape(n_dec, GROUP),
                             v_all[l].reshape(n_dec, GROUP), gatt, row(subln_w), lamv[l],
                             ck, cv, l, DECODE_PAGES)
        hs = _outproj(hs, y_rec[:, 0, :], y_att, w_out_bf[l], fw, final, n_dec)
        ss.append(s_s)

    n_ah = GROUP // HEAD
    y_prompt = hp.reshape(bp, seq, D_MODEL)[:, N_META:]
    y_sample = hs.reshape(n_dec, 1, D_MODEL)
    return (y_prompt, y_sample,
            kv_p[0].reshape(depth, bp, seq, n_ah, HEAD), kv_p[1].reshape(depth, bp, seq, n_ah, HEAD),
            jnp.stack(sp),
            kv_s[0].reshape(depth, n_dec, 1, n_ah, HEAD), kv_s[1].reshape(depth, n_dec, 1, n_ah, HEAD),
            jnp.stack(ss))
```

```python
import functools
import math

import jax
import jax.numpy as jnp
from jax import lax
from jax.experimental import pallas as pl
from jax.experimental.pallas import tpu as pltpu

F32 = jnp.float32
BF16 = jnp.bfloat16

D_MODEL = 1024
N_META = 16
GROUP = 512
HEAD = 128
N_HEADS = 4
ATT_DH = 64
PAGE = 128
CHUNK = 64
SUB = 16
EPS = 1e-6
LOG2E = 1.4426950408889634
VMEM_LIMIT = 52 * 1024 * 1024

NT_DIMS = (((1,), (1,)), ((), ()))
TN_DIMS = (((0,), (0,)), ((), ()))


def _lam_init(layer):
    return 0.8 - 0.6 * math.exp(-0.3 * layer)


def _sigmoid(x):
    return 1.0 / (1.0 + jnp.exp(-x))


def _inproj_kernel(*refs, aliased):
    if aliased:
        refs = refs[:6] + refs[8:]
    (x_ref, nw_ref, w_ref, loglb_ref, log1mlb_ref, omlb_ref,
     rq_ref, logf_ref, kin_ref, ri_ref, grec_ref, aq_ref, k_ref, v_ref, kbf_ref, vbf_ref,
     gatt_ref) = refs
    x = x_ref[...]
    ms = jnp.mean(x * x, axis=-1, keepdims=True)
    xn = (x * lax.rsqrt(ms + EPS) * nw_ref[...]).astype(BF16)

    def proj(g):
        return jnp.dot(xn, w_ref[:, g * GROUP:(g + 1) * GROUP], preferred_element_type=F32)

    rq_ref[...] = proj(0).astype(BF16)
    z = proj(1)
    e = jnp.exp(-jnp.abs(z))
    log_sig = jnp.minimum(z, 0.0) - jnp.log(1.0 + e)
    a = loglb_ref[...]
    b = log1mlb_ref[...] + log_sig
    logf_ref[...] = jnp.maximum(a, b) + jnp.log(1.0 + jnp.exp(-jnp.abs(a - b)))
    kin_ref[...] = omlb_ref[...] * (jnp.where(z >= 0.0, e, 1.0) / (1.0 + e))
    ri_ref[...] = proj(2).astype(BF16)
    g = proj(3)
    grec_ref[...] = g * _sigmoid(g)
    aq_ref[...] = (proj(4) * (ATT_DH ** -0.5)).astype(BF16)
    tm = x.shape[0]
    k = proj(5)
    v = proj(6)
    for h in range(N_HEADS):
        k_ref[pl.ds(h, tm, stride=N_HEADS), :] = k[:, h * HEAD:(h + 1) * HEAD]
        v_ref[pl.ds(h, tm, stride=N_HEADS), :] = v[:, h * HEAD:(h + 1) * HEAD]
    kbf_ref[...] = k.astype(BF16)
    vbf_ref[...] = v.astype(BF16)
    g = proj(7)
    gatt_ref[...] = g * _sigmoid(g)


def _inproj(h, nw, w, loglb, log1mlb, omlb, layer, depth, kv_bufs, tm):
    m = h.shape[0]
    grid = (m // tm,)
    row = lambda i: (i, 0)
    const = lambda i: (0, 0)
    in_specs = [
        pl.BlockSpec((tm, D_MODEL), row),
        pl.BlockSpec((1, D_MODEL), const),
        pl.BlockSpec((D_MODEL, 8 * GROUP), const),
        pl.BlockSpec((1, GROUP), const),
        pl.BlockSpec((1, GROUP), const),
        pl.BlockSpec((1, GROUP), const),
    ]
    args = [h, nw, w, loglb, log1mlb, omlb]
    aliases = {}
    if kv_bufs is not None:
        in_specs += [pl.BlockSpec(memory_space=pl.ANY)] * 2
        args += list(kv_bufs)
        aliases = {6: 6, 7: 7}
    tile = lambda dt: jax.ShapeDtypeStruct((m, GROUP), dt)
    stacked = jax.ShapeDtypeStruct((depth, m * N_HEADS, HEAD), F32)
    out_shape = [tile(BF16), tile(F32), tile(F32), tile(BF16), tile(F32), tile(BF16),
                 stacked, stacked, tile(BF16), tile(BF16), tile(F32)]
    tile_spec = pl.BlockSpec((tm, GROUP), row)
    stacked_spec = pl.BlockSpec((None, tm * N_HEADS, HEAD), lambda i: (layer, i, 0))
    out_specs = [tile_spec] * 6 + [stacked_spec] * 2 + [tile_spec] * 3
    return pl.pallas_call(
        functools.partial(_inproj_kernel, aliased=kv_bufs is not None),
        grid=grid, in_specs=in_specs, out_specs=out_specs, out_shape=out_shape,
        input_output_aliases=aliases,
        compiler_params=pltpu.CompilerParams(
            dimension_semantics=("parallel",), vmem_limit_bytes=VMEM_LIMIT),
        name=f"inproj_m{m}",
    )(*args)


def _cumsum_rows(tril, g):
    g1 = g.astype(BF16)
    r1 = g - g1.astype(F32)
    g2 = r1.astype(BF16)
    g3 = (r1 - g2.astype(F32)).astype(BF16)
    dot = lambda x: jnp.dot(tril, x, preferred_element_type=F32)
    return dot(g1) + dot(g2) + dot(g3)


def _hgrn_kernel(q_ref, g_ref, k_ref, v_ref, gate_ref, s0_ref, nw_ref, y_ref, sout_ref,
                 st_ref, cj_ref, vf_ref, *, seq, heads):
    first = seq % CHUNK
    n_full = seq // CHUNK
    nw = nw_ref[...]
    row8 = lax.broadcasted_iota(jnp.int32, (8, 1), 0)

    def chunk(start, c):
        sl = pl.ds(start, c)
        r_i = lax.broadcasted_iota(jnp.int32, (c, c), 0)
        c_i = lax.broadcasted_iota(jnp.int32, (c, c), 1)
        tril = (r_i >= c_i).astype(BF16)
        hs = range(heads)
        cols = [slice(hh * HEAD, (hh + 1) * HEAD) for hh in hs]
        b2 = [_cumsum_rows(tril, g_ref[sl, cols[hh]]) * LOG2E for hh in hs]
        q = [q_ref[sl, cols[hh]].astype(F32) for hh in hs]
        k = [k_ref[sl, cols[hh]].astype(F32) for hh in hs]
        v = [v_ref[sl, cols[hh]] for hh in hs]
        o = [lax.dot_general((q[hh] * jnp.exp2(b2[hh])).astype(BF16), st_ref[hh].astype(BF16),
                             NT_DIMS, preferred_element_type=F32) for hh in hs]
        for hh in hs:
            cj_ref[hh, 0:c, :] = b2[hh] - jnp.log2(k[hh])
            vf_ref[hh, 0:c, :] = v[hh].astype(F32)
        off = [[None] * (c // SUB) for _ in hs]
        for s in range(1, c // SUB):
            lo = s * SUB
            for hh in hs:
                edge = b2[hh][lo - 1:lo]
                qh = (q[hh][lo:lo + SUB] * jnp.exp2(b2[hh][lo:lo + SUB] - edge)).astype(BF16)
                kh = (k[hh][:lo] * jnp.exp2(edge - b2[hh][:lo])).astype(BF16)
                a = lax.dot_general(qh, kh, NT_DIMS, preferred_element_type=F32)
                off[hh][s] = jnp.dot(a.astype(BF16), v[hh][:lo], preferred_element_type=F32)
        for hh in hs:
            parts = []
            for r in range(c // 8):
                b_r = b2[hh][8 * r:8 * r + 8]
                q_r = q[hh][8 * r:8 * r + 8]
                o_r = o[hh][8 * r:8 * r + 8]
                lo = (8 * r // SUB) * SUB
                if lo > 0:
                    o_r = o_r + off[hh][lo // SUB][8 * r - lo:8 * r - lo + 8]
                for jj in range(lo, 8 * r + 8):
                    col = jnp.sum(q_r * jnp.exp2(b_r - cj_ref[hh, jj:jj + 1, :]),
                                  axis=-1, keepdims=True)
                    if jj >= 8 * r:
                        col = jnp.where(row8 >= jj - 8 * r, col, 0.0)
                    o_r = o_r + col * vf_ref[hh, jj:jj + 1, :]
                parts.append(o_r)
            oo = jnp.concatenate(parts, axis=0)
            ms = jnp.mean(oo * oo, axis=-1, keepdims=True)
            y = oo * lax.rsqrt(ms + EPS) * nw * gate_ref[sl, cols[hh]].astype(F32)
            y_ref[sl, cols[hh]] = y.astype(y_ref.dtype)
        for hh in hs:
            b_last = b2[hh][c - 1:c]
            kd = (k[hh] * jnp.exp2(b_last - b2[hh])).astype(BF16)
            upd = lax.dot_general(v[hh], kd, TN_DIMS, preferred_element_type=F32)
            st_ref[hh] = st_ref[hh] * jnp.exp2(b_last) + upd

    for hh in range(heads):
        st_ref[hh] = s0_ref[hh].T
    if first:
        chunk(0, first)
    if n_full:
        def body(j, carry):
            chunk(pl.multiple_of(first + j * CHUNK, SUB), CHUNK)
            return carry
        lax.fori_loop(0, n_full, body, 0)
    for hh in range(heads):
        sout_ref[hh] = st_ref[hh].T


def _hgrn(rq, logf, kin, ri, grec, s0, nw, heads, name):
    bsz, seq, _ = rq.shape
    assert seq % SUB == 0 and N_HEADS % heads == 0
    tok = pl.BlockSpec((None, seq, heads * HEAD), lambda b, h: (b, 0, h))
    state = pl.BlockSpec((None, heads, HEAD, HEAD), lambda b, h: (b, h, 0, 0))
    return pl.pallas_call(
        functools.partial(_hgrn_kernel, seq=seq, heads=heads),
        grid=(bsz, N_HEADS // heads),
        in_specs=[tok, tok, tok, tok, tok, state, pl.BlockSpec((1, HEAD), lambda b, h: (0, 0))],
        out_specs=[tok, state],
        out_shape=[jax.ShapeDtypeStruct((bsz, seq, GROUP), BF16),
                   jax.ShapeDtypeStruct(s0.shape, F32)],
        scratch_shapes=[pltpu.VMEM((heads, HEAD, HEAD), F32), pltpu.VMEM((heads, CHUNK, HEAD), F32),
                        pltpu.VMEM((heads, CHUNK, HEAD), F32)],
        compiler_params=pltpu.CompilerParams(
            dimension_semantics=("parallel", "parallel"), vmem_limit_bytes=VMEM_LIMIT),
        name=name,
    )(rq, logf, kin, ri, grec, s0, nw)


def _lam(lamv, lam_init):
    return (jnp.exp(jnp.sum(lamv[0:1] * lamv[1:2], axis=-1, keepdims=True))
            - jnp.exp(jnp.sum(lamv[2:3] * lamv[3:4], axis=-1, keepdims=True)) + lam_init)


def _split_components(q):
    lane = lax.broadcasted_iota(jnp.int32, q.shape, 1)
    qf = q.astype(F32)
    return jnp.concatenate([jnp.where(lane < ATT_DH, qf, 0.0), jnp.where(lane >= ATT_DH, qf, 0.0)],
                           axis=0).astype(BF16)


def _attn_prompt_kernel(q_ref, k_ref, v_ref, gate_ref, sw_ref, lamv_ref, y_ref,
                        qp_ref, m_ref, l_ref, acc_ref, *, tq, lam_init):
    i = pl.program_id(1)
    lam = _lam(lamv_ref[...], lam_init)
    sw = sw_ref[...]
    heads = range(N_HEADS)
    cols = [slice(h * HEAD, (h + 1) * HEAD) for h in heads]

    def causal(t):
        r_i = lax.broadcasted_iota(jnp.int32, (2 * t, t), 0)
        c_i = lax.broadcasted_iota(jnp.int32, (2 * t, t), 1)
        return c_i <= jnp.where(r_i >= t, r_i - t, r_i)

    def start(q_rows, t):
        for h in heads:
            qp_ref[h, 0:2 * t, :] = _split_components(q_ref[q_rows, cols[h]])
            m_ref[h, 0:2 * t, :] = jnp.full((2 * t, HEAD), -jnp.inf, F32)
            l_ref[h, 0:2 * t, :] = jnp.zeros((2 * t, HEAD), F32)
            acc_ref[h, 0:2 * t, :] = jnp.zeros((2 * t, HEAD), F32)

    def attend(k_rows, t, mask=None):
        rows = slice(0, 2 * t)
        s = [lax.dot_general(qp_ref[h, rows, :], k_ref[k_rows, cols[h]], NT_DIMS,
                             preferred_element_type=F32) for h in heads]
        p, alpha = [], []
        for h in heads:
            sh = s[h] if mask is None else jnp.where(mask, s[h], -jnp.inf)
            blocks = [sh[:, c:c + HEAD] for c in range(0, sh.shape[1], HEAD)]
            m_old = m_ref[h, rows, :]
            m_new = jnp.maximum(m_old, jnp.max(sh, axis=-1, keepdims=True))
            a = jnp.exp(m_old - m_new)
            ph = [jnp.exp(blk - m_new[:, :blk.shape[1]]) for blk in blocks]
            tot = ph[0]
            for blk in ph[1:]:
                tot = tot + blk
            l_ref[h, rows, :] = a * l_ref[h, rows, :] + jnp.sum(tot, axis=-1, keepdims=True)
            m_ref[h, rows, :] = m_new
            p.append(jnp.concatenate(ph, axis=1).astype(BF16) if len(ph) > 1
                     else ph[0].astype(BF16))
            alpha.append(a)
        for h in heads:
            acc_ref[h, rows, :] = alpha[h] * acc_ref[h, rows, :] + jnp.dot(
                p[h], v_ref[k_rows, cols[h]], preferred_element_type=F32)

    def finish(q_rows, t):
        for h in heads:
            acc = acc_ref[h, 0:2 * t, :]
            l = l_ref[h, 0:2 * t, :]
            o = acc[:t] / l[:t] - lam * (acc[t:] / l[t:])
            ms = jnp.mean(o * o, axis=-1, keepdims=True)
            y = o * lax.rsqrt(ms + EPS) * sw * (1.0 - lam_init) * gate_ref[q_rows, cols[h]]
            y_ref[q_rows, cols[h]] = y.astype(y_ref.dtype)

    meta = pl.ds(0, N_META)

    @pl.when(i == 0)
    def _():
        start(meta, N_META)
        attend(meta, N_META, causal(N_META))
        finish(meta, N_META)

    q_rows = pl.ds(pl.multiple_of(N_META + i * tq, N_META), tq)
    start(q_rows, tq)
    attend(meta, tq)

    def body(j, carry):
        attend(pl.ds(pl.multiple_of(N_META + j * tq, N_META), tq), tq)
        return carry

    lax.fori_loop(0, i, body, 0)
    attend(q_rows, tq, causal(tq))
    finish(q_rows, tq)


def _attn_prompt(aq, kbf, vbf, gatt, sw, lamv, layer, tq):
    bsz, seq, _ = aq.shape
    assert (seq - N_META) % tq == 0
    tok = pl.BlockSpec((None, seq, GROUP), lambda b, i: (b, 0, 0))
    const = lambda b, i: (0, 0)
    return pl.pallas_call(
        functools.partial(_attn_prompt_kernel, tq=tq, lam_init=_lam_init(layer)),
        grid=(bsz, (seq - N_META) // tq),
        in_specs=[tok, tok, tok, tok, pl.BlockSpec((1, HEAD), const),
                  pl.BlockSpec((4, ATT_DH), const)],
        out_specs=tok,
        out_shape=jax.ShapeDtypeStruct((bsz, seq, GROUP), BF16),
        scratch_shapes=[pltpu.VMEM((N_HEADS, 2 * tq, HEAD), BF16),
                        pltpu.VMEM((N_HEADS, 2 * tq, HEAD), F32),
                        pltpu.VMEM((N_HEADS, 2 * tq, HEAD), F32),
                        pltpu.VMEM((N_HEADS, 2 * tq, HEAD), F32)],
        compiler_params=pltpu.CompilerParams(
            dimension_semantics=("parallel", "arbitrary"), vmem_limit_bytes=VMEM_LIMIT),
        name="attn_prompt",
    )(aq, kbf, vbf, gatt, sw, lamv)


def _attn_decode_kernel(pt_ref, q_ref, kn_ref, vn_ref, gate_ref, sw_ref, lamv_ref, *rest,
                        pages, lam_init):
    k_refs = rest[:pages]
    v_refs = rest[pages:2 * pages]
    y_ref = rest[2 * pages]
    q8_ref, m_ref, l_ref, acc_ref = rest[2 * pages + 1:]
    g = pl.program_id(1)
    n_comp = 2 * N_HEADS

    @pl.when(g == 0)
    def _():
        q = jnp.broadcast_to(q_ref[...].astype(F32), (n_comp, GROUP))
        lane = lax.broadcasted_iota(jnp.int32, (n_comp, GROUP), 1)
        lo = lax.broadcasted_iota(jnp.int32, (n_comp, GROUP), 0) * ATT_DH
        q8 = jnp.where((lane >= lo) & (lane < lo + ATT_DH), q, 0.0).astype(BF16)
        q8_ref[...] = q8
        kn = jnp.broadcast_to(kn_ref[...].astype(BF16), (n_comp, GROUP))
        s_new = lax.dot_general(kn, q8, NT_DIMS, preferred_element_type=F32)
        m_ref[...] = s_new[0:1]
        l_ref[...] = jnp.ones_like(l_ref)
        row0 = lax.broadcasted_iota(jnp.int32, (8, HEAD), 0) == 0
        for j in range(n_comp):
            h = j // 2
            vn = jnp.broadcast_to(vn_ref[:, h * HEAD:(h + 1) * HEAD], (8, HEAD))
            acc_ref[j] = jnp.where(row0, vn, 0.0)

    def head_rows(ref, h):
        return ref[pl.ds(h, PAGE, stride=N_HEADS), :]

    q8 = q8_ref[...]
    scores = []
    for p in range(pages):
        kb = jnp.concatenate([head_rows(k_refs[p], h) for h in range(N_HEADS)], axis=1)
        scores.append(lax.dot_general(kb.astype(BF16), q8, NT_DIMS,
                                      preferred_element_type=F32))
    m_old = m_ref[...]
    m_new = m_old
    for s in scores:
        m_new = jnp.maximum(m_new, jnp.max(s, axis=0, keepdims=True))
    alpha = jnp.exp(m_old - m_new)
    probs = [jnp.exp(s - m_new) for s in scores]
    l_new = alpha * l_ref[...]
    for p in probs:
        l_new = l_new + jnp.sum(p, axis=0, keepdims=True)
    m_ref[...] = m_new
    l_ref[...] = l_new
    p_all = jnp.concatenate(probs, axis=0).astype(BF16)
    sel_row = lax.broadcasted_iota(jnp.int32, (n_comp, 2 * HEAD), 0)
    sel_col = lax.broadcasted_iota(jnp.int32, (n_comp, 2 * HEAD), 1)
    for h in range(N_HEADS):
        sel = jnp.where(sel_row == jnp.where(sel_col < HEAD, 2 * h, 2 * h + 1), 1.0, 0.0)
        pb = jnp.dot(p_all, sel.astype(BF16), preferred_element_type=F32)
        part = [acc_ref[2 * h + c] * alpha[:, 2 * h + c:2 * h + c + 1] for c in range(2)]
        for p in range(pages):
            vh = head_rows(v_refs[p], h)
            for c in range(2):
                pv = pb[p * PAGE:(p + 1) * PAGE, c * HEAD:(c + 1) * HEAD] * vh
                part[c] = part[c] + jnp.sum(pv.reshape(PAGE // 8, 8, HEAD), axis=0)
        for c in range(2):
            acc_ref[2 * h + c] = part[c]

    @pl.when(g == pl.num_programs(1) - 1)
    def _():
        lam = _lam(lamv_ref[...], lam_init)
        l = l_ref[...]
        sw = sw_ref[...]
        for h in range(N_HEADS):
            o1 = jnp.sum(acc_ref[2 * h], axis=0, keepdims=True) / l[:, 2 * h:2 * h + 1]
            o2 = jnp.sum(acc_ref[2 * h + 1], axis=0, keepdims=True) / l[:, 2 * h + 1:2 * h + 2]
            o = o1 - lam * o2
            ms = jnp.mean(o * o, axis=-1, keepdims=True)
            cols = slice(h * HEAD, (h + 1) * HEAD)
            y = o * lax.rsqrt(ms + EPS) * sw * (1.0 - lam_init) * gate_ref[:, cols]
            y_ref[:, cols] = y.astype(y_ref.dtype)


def _attn_decode(page_table, aq, kn, vn, gatt, sw, lamv, cache_k, cache_v, layer, pages):
    n_seq, n_pages = page_table.shape
    assert n_pages % pages == 0
    tok = pl.BlockSpec((None, 1, GROUP), lambda b, g, pt: (b, 0, 0))
    const = lambda b, g, pt: (0, 0)

    def page_spec(p):
        return pl.BlockSpec((None, None, PAGE * N_HEADS, HEAD),
                            lambda b, g, pt: (layer, pt[b, g * pages + p], 0, 0))

    n_comp = 2 * N_HEADS
    grid_spec = pltpu.PrefetchScalarGridSpec(
        num_scalar_prefetch=1,
        grid=(n_seq, n_pages // pages),
        in_specs=[tok, tok, tok, tok, pl.BlockSpec((1, HEAD), const),
                  pl.BlockSpec((4, ATT_DH), const)]
                 + [page_spec(p) for p in range(pages)] * 2,
        out_specs=tok,
        scratch_shapes=[pltpu.VMEM((n_comp, GROUP), BF16), pltpu.VMEM((1, n_comp), F32),
                        pltpu.VMEM((1, n_comp), F32), pltpu.VMEM((n_comp, 8, HEAD), F32)],
    )
    r3 = lambda a: a.reshape(n_seq, 1, GROUP)
    return pl.pallas_call(
        functools.partial(_attn_decode_kernel, pages=pages, lam_init=_lam_init(layer)),
        grid_spec=grid_spec,
        out_shape=jax.ShapeDtypeStruct((n_seq, 1, GROUP), BF16),
        compiler_params=pltpu.CompilerParams(
            dimension_semantics=("parallel", "arbitrary"), vmem_limit_bytes=VMEM_LIMIT),
        name="attn_decode",
    )(page_table, r3(aq), r3(kn), r3(vn), r3(gatt), sw, lamv,
      *([cache_k] * pages), *([cache_v] * pages)).reshape(n_seq, GROUP)


def _outproj_kernel(h_ref, yr_ref, ya_ref, w_ref, fw_ref, o_ref, *, final):
    h = (h_ref[...]
         + jnp.dot(yr_ref[...], w_ref[:GROUP, :], preferred_element_type=F32)
         + jnp.dot(ya_ref[...], w_ref[GROUP:, :], preferred_element_type=F32))
    if final:
        ms = jnp.mean(h * h, axis=-1, keepdims=True)
        h = h * lax.rsqrt(ms + EPS) * fw_ref[...]
    o_ref[...] = h


def _outproj(h, y_rec, y_att, w, fw, final, tm):
    m = h.shape[0]
    row = lambda i: (i, 0)
    const = lambda i: (0, 0)
    return pl.pallas_call(
        functools.partial(_outproj_kernel, final=final),
        grid=(m // tm,),
        in_specs=[pl.BlockSpec((tm, D_MODEL), row), pl.BlockSpec((tm, GROUP), row),
                  pl.BlockSpec((tm, GROUP), row), pl.BlockSpec((2 * GROUP, D_MODEL), const),
                  pl.BlockSpec((1, D_MODEL), const)],
        out_specs=pl.BlockSpec((tm, D_MODEL), row),
        out_shape=jax.ShapeDtypeStruct((m, D_MODEL), F32),
        compiler_params=pltpu.CompilerParams(
            dimension_semantics=("parallel",), vmem_limit_bytes=VMEM_LIMIT),
        name=f"outproj_m{m}",
    )(h, y_rec, y_att, w, fw)


PROMPT_TM = 384
PROMPT_TQ = 256
HGRN_HEADS = 4
DECODE_PAGES = 16


def kernel(x_prompt, x_sample, cache_k, cache_v, state_hgrn, page_table, meta_tokens, norm_w, w_in,
           w_out, lb_logits, hgrn_norm_w, subln_w, lam_q1, lam_k1, lam_q2, lam_k2, final_norm_w):
    depth = w_in.shape[0]
    bp, seq_x, _ = x_prompt.shape
    seq = seq_x + N_META
    n_dec = x_sample.shape[0]
    n_phys = cache_k.shape[1]

    p = jax.nn.softmax(lb_logits.astype(F32), axis=0)
    c = jnp.cumsum(p, axis=0)
    lb = c - c[0:1]
    loglb, log1mlb, omlb = jnp.log(lb), jnp.log1p(-lb), 1.0 - lb
    w_in_bf = w_in.astype(BF16)
    w_out_bf = w_out.astype(BF16)
    lamv = jnp.stack([lam_q1, lam_k1, lam_q2, lam_k2], axis=1).astype(F32)
    fw = final_norm_w.reshape(1, D_MODEL)
    ck = cache_k.reshape(depth, n_phys, PAGE * N_HEADS, HEAD)
    cv = cache_v.reshape(depth, n_phys, PAGE * N_HEADS, HEAD)

    meta = jnp.broadcast_to(meta_tokens[None].astype(x_prompt.dtype), (bp, N_META, D_MODEL))
    hp = jnp.concatenate([meta, x_prompt], axis=1).reshape(bp * seq, D_MODEL)
    hs = x_sample.reshape(n_dec, D_MODEL)
    zero_state = jnp.zeros((bp, N_HEADS, HEAD, HEAD), F32)

    kv_p = None
    kv_s = None
    sp, ss = [], []
    for l in range(depth):
        row = lambda a: a[l].reshape(1, -1)
        par = (row(norm_w), w_in_bf[l], row(loglb), row(log1mlb), row(omlb))
        final = l == depth - 1

        rq, logf, kin, ri, grec, aq, k_all, v_all, kbf, vbf, gatt = _inproj(
            hp, *par, l, depth, kv_p, PROMPT_TM)
        kv_p = (k_all, v_all)
        b3 = lambda a: a.reshape(bp, seq, GROUP)
        y_rec, s_p = _hgrn(b3(rq), b3(logf), b3(kin), b3(ri), b3(grec), zero_state,
                           row(hgrn_norm_w), HGRN_HEADS, "hgrn_prompt")
        y_att = _attn_prompt(b3(aq), b3(kbf), b3(vbf), b3(gatt), row(subln_w), lamv[l], l,
                             PROMPT_TQ)
        hp = _outproj(hp, y_rec.reshape(bp * seq, GROUP), y_att.reshape(bp * seq, GROUP),
                      w_out_bf[l], fw, final, PROMPT_TM)
        sp.append(s_p)

        rq, logf, kin, ri, grec, aq, k_all, v_all, kbf, vbf, gatt = _inproj(
            hs, *par, l, depth, kv_s, n_dec)
        kv_s = (k_all, v_all)
        pad = lambda a: jnp.pad(a.reshape(n_dec, 1, GROUP), ((0, 0), (0, SUB - 1), (0, 0)))
        y_rec, s_s = _hgrn(pad(rq), pad(logf), pad(kin), pad(ri), pad(grec), state_hgrn[l],
                           row(hgrn_norm_w), HGRN_HEADS, "hgrn_decode")
        y_att = _attn_decode(page_table, aq, k_all[l].reshape(n_dec, GROUP),
                             v_all[l].reshape(n_dec, GROUP), gatt, row(subln_w), lamv[l],
                             ck, cv, l, DECODE_PAGES)
        hs = _outproj(hs, y_rec[:, 0, :], y_att, w_out_bf[l], fw, final, n_dec)
        ss.append(s_s)

    n_ah = GROUP // HEAD
    y_prompt = hp.reshape(bp, seq, D_MODEL)[:, N_META:]
    y_sample = hs.reshape(n_dec, 1, D_MODEL)
    return (y_prompt, y_sample,
            kv_p[0].reshape(depth, bp, seq, n_ah, HEAD), kv_p[1].reshape(depth, bp, seq, n_ah, HEAD),
            jnp.stack(sp),
            kv_s[0].reshape(depth, n_dec, 1, n_ah, HEAD), kv_s[1].reshape(depth, n_dec, 1, n_ah, HEAD),
            jnp.stack(ss))
```
